```python
import functools
import jax, jax.numpy as jnp
from jax import lax
import numpy as np

D_MODEL = 1024
BATCH = 16
SEQ = 2048
DEPTH = 2
DEC_BATCH = 32
DEC_SEQ = 8
PAST_LEN = 16384
PAGE_SIZE = 128

BRANCH_WIDTH = D_MODEL // 2
POOL_WINDOWS = (2, 4, 8, 16)
POOL_GROUPS = len(POOL_WINDOWS)
POOL_GROUP_WIDTH = BRANCH_WIDTH // POOL_GROUPS
POOL_STATE = max(POOL_WINDOWS) - 1
LRU_BLOCKS = 8
LRU_BLOCK_WIDTH = BRANCH_WIDTH // LRU_BLOCKS
CONV_WIDTH = 4
LRU_C = 8.0
ATT_HEADS = 8
ATT_HEAD_DIM = BRANCH_WIDTH // ATT_HEADS
Q_BLOCK = 128
MEM_LEN = 256
MEM_HEADS = 4
MEM_HEAD_DIM = BRANCH_WIDTH // MEM_HEADS
N_BRANCH = 4
N_GROUPS = 4
EXPERTS_PER_GROUP = 8
N_EXPERTS = N_GROUPS * EXPERTS_PER_GROUP
TOP_K_IN_GROUP = 2
D_EXPERT = D_MODEL // 4
EPS = 1e-6
FORGET_BIAS = 3.0

IN_SIZES = (BRANCH_WIDTH, BRANCH_WIDTH, BRANCH_WIDTH, BRANCH_WIDTH, BRANCH_WIDTH, BRANCH_WIDTH,
            ATT_HEADS, BRANCH_WIDTH, N_BRANCH * D_MODEL)
IN_SPLITS = tuple(int(s) for s in np.cumsum(IN_SIZES)[:-1])
D_IN = int(sum(IN_SIZES))

kernel_name = 'hybrid_pool_lru_fox_hmoe_step'


def rms_norm(x, g):
    xf = x.astype(jnp.float32)
    xf = xf * lax.rsqrt(jnp.mean(xf * xf, axis=-1, keepdims=True) + EPS)
    return (xf * g.astype(jnp.float32)).astype(x.dtype)


def pool_mix(u, prev, pos0, w_grp, scale):
    B, L, C = u.shape
    full = jnp.concatenate([prev.astype(u.dtype), u], axis=1)
    cs = jnp.cumsum(full.astype(jnp.float32), axis=1)
    cs = jnp.pad(cs, ((0, 0), (1, 0), (0, 0)))
    end = cs[:, POOL_STATE + 1:]
    pos = pos0 + jnp.arange(L)
    means = []
    for g, w in enumerate(POOL_WINDOWS):
        sl = slice(g * POOL_GROUP_WIDTH, (g + 1) * POOL_GROUP_WIDTH)
        start = cs[:, POOL_STATE + 1 - w:POOL_STATE + 1 - w + L, sl]
        cnt = jnp.minimum(w, pos + 1).astype(jnp.float32)[None, :, None]
        means.append((end[..., sl] - start) / cnt)
    d = (jnp.concatenate(means, axis=-1) - u.astype(jnp.float32)).astype(u.dtype)
    d = d.reshape(B, L, POOL_GROUPS, POOL_GROUP_WIDTH)
    y = jnp.einsum('blgc,gcd->blgd', d, w_grp).reshape(B, L, C) * scale
    return y, full[:, -POOL_STATE:]


def _linear_combine(left, right):
    a_l, b_l = left
    a_r, b_r = right
    return a_l * a_r, a_r * b_l + b_r


def rglru_branch(x_in, gate_in, conv_prev, h_prev, pos0, conv_w, conv_b, wa, ba, wx, bx, lam):
    B, L, C = x_in.shape
    full = jnp.concatenate([conv_prev.astype(x_in.dtype), x_in], axis=1)
    xc = conv_b
    for j in range(CONV_WIDTH):
        xc = xc + full[:, j:j + L] * conv_w[j]
    xr = xc.reshape(B, L, LRU_BLOCKS, LRU_BLOCK_WIDTH)
    r = jax.nn.sigmoid((jnp.einsum('blhi,hij->blhj', xr, wa).reshape(B, L, C) + ba).astype(jnp.float32))
    i = jax.nn.sigmoid((jnp.einsum('blhi,hij->blhj', xr, wx).reshape(B, L, C) + bx).astype(jnp.float32))
    log_a = -LRU_C * r * jax.nn.softplus(-lam.astype(jnp.float32))
    a = jnp.exp(log_a)
    pos = pos0 + jnp.arange(L)
    mult = jnp.where((pos == 0)[None, :, None], 1.0, jnp.sqrt(-jnp.expm1(2.0 * log_a)))
    b = mult * i * xc.astype(jnp.float32)
    b = b.at[:, 0].add(a[:, 0] * h_prev.astype(jnp.float32))
    _, h = lax.associative_scan(_linear_combine, (a, b), axis=1)
    y = (h * jax.nn.gelu(gate_in.astype(jnp.float32))).astype(x_in.dtype)
    return y, full[:, -(CONV_WIDTH - 1):], h[:, -1]


def fox_prompt(q, k, v, lf):
    B, S, H, Dh = q.shape
    c = jnp.swapaxes(jnp.cumsum(lf, axis=1), 1, 2)
    outs = []
    for blk in range(S // Q_BLOCK):
        q0, q1 = blk * Q_BLOCK, (blk + 1) * Q_BLOCK
        s = jnp.einsum('bqhd,bkhd->bhqk', q[:, q0:q1], k[:, :q1]).astype(jnp.float32) * ATT_HEAD_DIM ** -0.5
        s = s + c[:, :, q0:q1, None] - c[:, :, None, :q1]
        causal = (q0 + jnp.arange(Q_BLOCK))[:, None] >= jnp.arange(q1)[None, :]
        s = jnp.where(causal, s, -jnp.inf)
        p = jax.nn.softmax(s, axis=-1).astype(v.dtype)
        outs.append(jnp.einsum('bhqk,bkhd->bqhd', p, v[:, :q1]))
    return jnp.concatenate(outs, axis=1)


def fox_sample(q, k, v, lf, k_past, v_past, lf_past):
    B, L, H, Dh = q.shape
    P = k_past.shape[1]
    scale = ATT_HEAD_DIM ** -0.5
    c_new = jnp.swapaxes(jnp.cumsum(lf, axis=1), 1, 2)
    suffix = lax.cumsum(lf_past.astype(jnp.float32), axis=1, reverse=True)
    d_past = jnp.concatenate([suffix[:, 1:], jnp.zeros_like(suffix[:, :1])], axis=1)
    d_past = jnp.swapaxes(d_past, 1, 2)
    s_past = (jnp.einsum('bqhd,bkhd->bhqk', q, k_past.astype(q.dtype)).astype(jnp.float32) * scale
              + c_new[..., :, None] + d_past[..., None, :])
    s_new = (jnp.einsum('bqhd,bkhd->bhqk', q, k).astype(jnp.float32) * scale
             + c_new[..., :, None] - c_new[..., None, :])
    causal = jnp.arange(L)[:, None] >= jnp.arange(L)[None, :]
    s_new = jnp.where(causal, s_new, -jnp.inf)
    p = jax.nn.softmax(jnp.concatenate([s_past, s_new], axis=-1), axis=-1).astype(v.dtype)
    return (jnp.einsum('bhqk,bkhd->bqhd', p[..., :P], v_past.astype(v.dtype))
            + jnp.einsum('bhqk,bkhd->bqhd', p[..., P:], v))


def memory_kv(mem, g, w_kv, k_gain):
    B, M, _ = mem.shape
    kv = rms_norm(mem, g) @ w_kv
    k, v = jnp.split(kv, 2, axis=-1)
    k = rms_norm(k.reshape(B, M, MEM_HEADS, MEM_HEAD_DIM), k_gain)
    return k, v.reshape(B, M, MEM_HEADS, MEM_HEAD_DIM)


def memory_attend(q, k, v):
    s = jnp.einsum('blhd,bmhd->bhlm', q, k.astype(q.dtype)).astype(jnp.float32) * MEM_HEAD_DIM ** -0.5
    p = jax.nn.softmax(s, axis=-1).astype(v.dtype)
    return jnp.einsum('bhlm,bmhd->blhd', p, v)


def hier_moe(x, w_group, b_group, w_router, b_router, w_gate, w_up, w_down):
    B, L, D = x.shape
    t = x.reshape(B * L, D)
    g_logit = (t @ w_group + b_group).astype(jnp.float32)
    g_onehot = jax.nn.one_hot(jnp.argmax(g_logit, axis=-1), N_GROUPS, dtype=jnp.float32)
    g_w = jnp.sum(jax.nn.softmax(g_logit, axis=-1) * g_onehot, axis=-1, keepdims=True)
    e_logit = (t @ w_router + b_router).astype(jnp.float32).reshape(-1, N_GROUPS, EXPERTS_PER_GROUP)
    e_logit = jnp.einsum('tge,tg->te', e_logit, g_onehot)
    top_v, top_i = lax.top_k(e_logit, TOP_K_IN_GROUP)
    top_w = jax.nn.softmax(top_v, axis=-1) * g_w
    w_in_group = jnp.sum(jax.nn.one_hot(top_i, EXPERTS_PER_GROUP, dtype=jnp.float32) * top_w[..., None], axis=1)
    combine = (g_onehot[:, :, None] * w_in_group[:, None, :]).astype(x.dtype)
    y = jnp.zeros_like(t)
    for gi in range(N_GROUPS):
        h = jax.nn.silu(jnp.einsum('td,edf->tef', t, w_gate[gi])) * jnp.einsum('td,edf->tef', t, w_up[gi])
        y = y + jnp.einsum('tef,efd->td', h * combine[:, gi, :, None], w_down[gi])
    return y.reshape(B, L, D)


def hybrid_layer(x, p, pos0, pool_prev, conv_prev, h_prev, mem_k, mem_v, attend):
    B, L, _ = x.shape
    u = rms_norm(x, p['norm_mix'])
    z = u @ p['w_in'] + p['b_in']
    pool_in, lru_in, lru_gate, q, k, v, f_logit, mq, gate_logit = jnp.split(z, IN_SPLITS, axis=-1)
    o_pool, pool_new = pool_mix(pool_in, pool_prev, pos0, p['pool_w'], p['pool_scale'])
    o_lru, conv_new, h_new = rglru_branch(lru_in, lru_gate, conv_prev, h_prev, pos0, p['conv_w'], p['conv_b'],
                                          p['lru_wa'], p['lru_ba'], p['lru_wx'], p['lru_bx'], p['lru_lambda'])
    q = rms_norm(q.reshape(B, L, ATT_HEADS, ATT_HEAD_DIM), p['q_norm'])
    k = rms_norm(k.reshape(B, L, ATT_HEADS, ATT_HEAD_DIM), p['k_norm'])
    v = v.reshape(B, L, ATT_HEADS, ATT_HEAD_DIM)
    lf = jax.nn.log_sigmoid(f_logit.astype(jnp.float32))
    o_att = attend(q, k, v, lf).reshape(B, L, BRANCH_WIDTH)
    mq = rms_norm(mq.reshape(B, L, MEM_HEADS, MEM_HEAD_DIM), p['mq_norm'])
    o_mem = memory_attend(mq, mem_k, mem_v).reshape(B, L, BRANCH_WIDTH)
    gates = jax.nn.sigmoid(gate_logit.reshape(B, L, N_BRANCH, D_MODEL))
    merged = None
    for n, o in enumerate((o_pool, o_lru, o_att, o_mem)):
        term = gates[:, :, n] * (o @ p['w_branch'][n])
        merged = term if merged is None else merged + term
    x = x + merged @ p['w_out']
    x = x + hier_moe(rms_norm(x, p['norm_ffn']), p['w_group'], p['b_group'], p['w_router'], p['b_router'],
                     p['w_gate'], p['w_up'], p['w_down'])
    return x, k, v, lf, pool_new, conv_new, h_new


def setup_inputs(seed: int = 0) -> dict:
    key = jax.random.key(seed)
    ks = iter(jax.random.split(key, 64))
    f32 = jnp.float32

    def nrm(shape, scale=1.0):
        return jax.random.normal(next(ks), shape, f32) * scale

    n_pages = PAST_LEN // PAGE_SIZE
    n_used = DEC_BATCH * n_pages
    n_pool = n_used + max(1, n_used // 4)
    perm = jax.random.permutation(next(ks), n_pool)
    page_table = perm[:n_used].reshape(DEC_BATCH, n_pages).astype(jnp.int32)

    f0, f1 = IN_SPLITS[5], IN_SPLITS[6]
    b_in = nrm((DEPTH, D_IN), 0.02).at[:, f0:f1].add(FORGET_BIAS)
    u = jax.random.uniform(next(ks), (DEPTH, BRANCH_WIDTH), f32, minval=0.9, maxval=0.999)
    a0 = u ** (1.0 / LRU_C)
    lru_lambda = jnp.log(a0) - jnp.log1p(-a0)

    return {
        'x_prompt': nrm((BATCH, SEQ, D_MODEL)),
        'x_sample': nrm((DEC_BATCH, DEC_SEQ, D_MODEL)),
        'cache_k': nrm((DEPTH, n_pool, PAGE_SIZE, ATT_HEADS, ATT_HEAD_DIM)),
        'cache_v': nrm((DEPTH, n_pool, PAGE_SIZE, ATT_HEADS, ATT_HEAD_DIM)),
        'cache_logf': jax.nn.log_sigmoid(FORGET_BIAS + nrm((DEPTH, n_pool, PAGE_SIZE, ATT_HEADS))),
        'cache_mem_k': nrm((DEPTH, DEC_BATCH, MEM_LEN, MEM_HEADS, MEM_HEAD_DIM)),
        'cache_mem_v': nrm((DEPTH, DEC_BATCH, MEM_LEN, MEM_HEADS, MEM_HEAD_DIM)),
        'state_pool': nrm((DEPTH, DEC_BATCH, POOL_STATE, BRANCH_WIDTH)),
        'state_conv': nrm((DEPTH, DEC_BATCH, CONV_WIDTH - 1, BRANCH_WIDTH)),
        'state_lru': nrm((DEPTH, DEC_BATCH, BRANCH_WIDTH), 0.5),
        'page_table': page_table,
        'mem_prompt': nrm((BATCH, MEM_LEN, D_MODEL)),
        'norm_mix': 1.0 + nrm((DEPTH, D_MODEL), 0.05),
        'w_in': nrm((DEPTH, D_MODEL, D_IN), D_MODEL ** -0.5),
        'b_in': b_in,
        'pool_w': nrm((DEPTH, POOL_GROUPS, POOL_GROUP_WIDTH, POOL_GROUP_WIDTH), POOL_GROUP_WIDTH ** -0.5),
        'pool_scale': 1.0 + nrm((DEPTH, BRANCH_WIDTH), 0.05),
        'conv_w': nrm((DEPTH, CONV_WIDTH, BRANCH_WIDTH), CONV_WIDTH ** -0.5),
        'conv_b': nrm((DEPTH, BRANCH_WIDTH), 0.02),
        'lru_wa': nrm((DEPTH, LRU_BLOCKS, LRU_BLOCK_WIDTH, LRU_BLOCK_WIDTH), LRU_BLOCK_WIDTH ** -0.5),
        'lru_ba': nrm((DEPTH, BRANCH_WIDTH), 0.02),
        'lru_wx': nrm((DEPTH, LRU_BLOCKS, LRU_BLOCK_WIDTH, LRU_BLOCK_WIDTH), LRU_BLOCK_WIDTH ** -0.5),
        'lru_bx': nrm((DEPTH, BRANCH_WIDTH), 0.02),
        'lru_lambda': lru_lambda,
        'q_norm': 1.0 + nrm((DEPTH, ATT_HEAD_DIM), 0.05),
        'k_norm': 1.0 + nrm((DEPTH, ATT_HEAD_DIM), 0.05),
        'mq_norm': 1.0 + nrm((DEPTH, MEM_HEAD_DIM), 0.05),
        'mem_norm': 1.0 + nrm((DEPTH, D_MODEL), 0.05),
        'w_mem_kv': nrm((DEPTH, D_MODEL, 2 * BRANCH_WIDTH), D_MODEL ** -0.5),
        'mk_norm': 1.0 + nrm((DEPTH, MEM_HEAD_DIM), 0.05),
        'w_branch': nrm((DEPTH, N_BRANCH, BRANCH_WIDTH, D_MODEL), BRANCH_WIDTH ** -0.5),
        'w_out': nrm((DEPTH, D_MODEL, D_MODEL), D_MODEL ** -0.5),
        'norm_ffn': 1.0 + nrm((DEPTH, D_MODEL), 0.05),
        'w_group': nrm((DEPTH, D_MODEL, N_GROUPS), D_MODEL ** -0.5),
        'b_group': nrm((DEPTH, N_GROUPS), 0.01),
        'w_router': nrm((DEPTH, D_MODEL, N_EXPERTS), D_MODEL ** -0.5),
        'b_router': nrm((DEPTH, N_EXPERTS), 0.01),
        'w_gate': nrm((DEPTH, N_GROUPS, EXPERTS_PER_GROUP, D_MODEL, D_EXPERT), D_MODEL ** -0.5),
        'w_up': nrm((DEPTH, N_GROUPS, EXPERTS_PER_GROUP, D_MODEL, D_EXPERT), D_MODEL ** -0.5),
        'w_down': nrm((DEPTH, N_GROUPS, EXPERTS_PER_GROUP, D_EXPERT, D_MODEL), D_EXPERT ** -0.5),
    }


def reference(x_prompt, x_sample, cache_k, cache_v, cache_logf, cache_mem_k, cache_mem_v,
              state_pool, state_conv, state_lru, page_table, mem_prompt,
              norm_mix, w_in, b_in, pool_w, pool_scale, conv_w, conv_b,
              lru_wa, lru_ba, lru_wx, lru_bx, lru_lambda, q_norm, k_norm, mq_norm,
              mem_norm, w_mem_kv, mk_norm, w_branch, w_out, norm_ffn,
              w_group, b_group, w_router, b_router, w_gate, w_up, w_down):
    past_len = page_table.shape[1] * PAGE_SIZE
    bp, bs = x_prompt.shape[0], x_sample.shape[0]
    pool0 = jnp.zeros((bp, POOL_STATE, BRANCH_WIDTH), x_prompt.dtype)
    conv0 = jnp.zeros((bp, CONV_WIDTH - 1, BRANCH_WIDTH), x_prompt.dtype)
    h0 = jnp.zeros((bp, BRANCH_WIDTH), jnp.float32)
    yp, ys = x_prompt, x_sample
    kp_l, vp_l, lfp_l, mkp_l, mvp_l, poolp_l, convp_l, lrup_l = [], [], [], [], [], [], [], []
    ks_l, vs_l, lfs_l, pools_l, convs_l, lrus_l = [], [], [], [], [], []
    for l in range(DEPTH):
        p = dict(norm_mix=norm_mix[l], w_in=w_in[l], b_in=b_in[l], pool_w=pool_w[l], pool_scale=pool_scale[l],
                 conv_w=conv_w[l], conv_b=conv_b[l], lru_wa=lru_wa[l], lru_ba=lru_ba[l], lru_wx=lru_wx[l],
                 lru_bx=lru_bx[l], lru_lambda=lru_lambda[l], q_norm=q_norm[l], k_norm=k_norm[l],
                 mq_norm=mq_norm[l], w_branch=w_branch[l], w_out=w_out[l], norm_ffn=norm_ffn[l],
                 w_group=w_group[l], b_group=b_group[l], w_router=w_router[l], b_router=b_router[l],
                 w_gate=w_gate[l], w_up=w_up[l], w_down=w_down[l])
        mk_p, mv_p = memory_kv(mem_prompt, mem_norm[l], w_mem_kv[l], mk_norm[l])
        yp, k_p, v_p, lf_p, pool_p, conv_p, h_p = hybrid_layer(yp, p, 0, pool0, conv0, h0, mk_p, mv_p, fox_prompt)
        kp_l.append(k_p); vp_l.append(v_p); lfp_l.append(lf_p); mkp_l.append(mk_p); mvp_l.append(mv_p)
        poolp_l.append(pool_p); convp_l.append(conv_p); lrup_l.append(h_p)
        k_past = cache_k[l][page_table].reshape(bs, past_len, ATT_HEADS, ATT_HEAD_DIM)
        v_past = cache_v[l][page_table].reshape(bs, past_len, ATT_HEADS, ATT_HEAD_DIM)
        lf_past = cache_logf[l][page_table].reshape(bs, past_len, ATT_HEADS)
        attend_s = functools.partial(fox_sample, k_past=k_past, v_past=v_past, lf_past=lf_past)
        ys, k_s, v_s, lf_s, pool_s, conv_s, h_s = hybrid_layer(ys, p, past_len, state_pool[l], state_conv[l],
                                                               state_lru[l], cache_mem_k[l], cache_mem_v[l], attend_s)
        ks_l.append(k_s); vs_l.append(v_s); lfs_l.append(lf_s)
        pools_l.append(pool_s); convs_l.append(conv_s); lrus_l.append(h_s)
    return (yp, ys,
            jnp.stack(kp_l), jnp.stack(vp_l), jnp.stack(lfp_l), jnp.stack(mkp_l), jnp.stack(mvp_l),
            jnp.stack(poolp_l), jnp.stack(convp_l), jnp.stack(lrup_l),
            jnp.stack(ks_l), jnp.stack(vs_l), jnp.stack(lfs_l),
            jnp.stack(pools_l), jnp.stack(convs_l), jnp.stack(lrus_l))
```

```python
import functools
import math

import jax
import jax.numpy as jnp
import numpy as np
from jax import lax
from jax.experimental import pallas as pl
from jax.experimental.pallas import tpu as pltpu

F32 = jnp.float32
BF16 = jnp.bfloat16

EPS = 1e-6
LRU_C = 8.0
POOL_WINDOWS = (2, 4, 8, 16)
POOL_HIST = 16
CONV_WIDTH = 4
CONV_HIST = 8
LANES = 128
SUBLANES_BF16 = 16
MXU_DIM = 256
VMEM_LIMIT = 56 * 1024 * 1024
TOKEN_TILE = 256
MOE_BLOCK = SUBLANES_BF16
MOE_CHUNK_BLOCKS = 16

NEG_INF = float("-inf")


def _dot(a, b):
    return jnp.dot(a, b, preferred_element_type=F32)


def _dot_nt(a, b):
    return lax.dot_general(a, b, (((1,), (1,)), ((), ())), preferred_element_type=F32)


def _dot_tn(a, b):
    return lax.dot_general(a, b, (((0,), (0,)), ((), ())), preferred_element_type=F32)


def _split3(x):
    hi = x.astype(BF16)
    r1 = x - hi.astype(F32)
    mid = r1.astype(BF16)
    lo = (r1 - mid.astype(F32)).astype(BF16)
    return hi, mid, lo


def _dot_exact_rhs01(x, m01):
    hi, mid, lo = _split3(x)
    return _dot(hi, m01) + _dot(mid, m01) + _dot(lo, m01)


def _parts(x):
    if isinstance(x, tuple):
        return x
    if x.dtype == BF16:
        return (x,)
    hi = x.astype(BF16)
    return hi, (x - hi.astype(F32)).astype(BF16)


def _mm(a, b, kind="nn"):
    pa, pb = _parts(a), _parts(b)
    f = {"nn": _dot, "nt": _dot_nt, "tn": _dot_tn}[kind]
    out = f(pa[0], pb[0])
    if len(pa) > 1:
        out = out + f(pa[1], pb[0])
    if len(pb) > 1:
        out = out + f(pa[0], pb[1])
    return out


def _wparts(ref, *idx):
    return tuple(ref[(p,) + idx] for p in range(ref.shape[0]))


def _round_to_bf16(w):
    return lax.reduce_precision(w, exponent_bits=8, mantissa_bits=7)


def _stack_parts(w, high_precision):
    hi = _round_to_bf16(w)
    if not high_precision:
        return hi.astype(BF16)[None]
    return jnp.stack([hi.astype(BF16), (w - hi).astype(BF16)])


def _softplus(y):
    return jnp.maximum(y, 0.0) + jnp.log1p(jnp.exp(-jnp.abs(y)))


def _gelu_tanh(x):
    return x * (0.5 * (1.0 + jnp.tanh(math.sqrt(2.0 / math.pi) * (x + 0.044715 * (x * x * x)))))


def _silu(x):
    return x * jax.nn.sigmoid(x)


def _rms(x, gain):
    ms = jnp.mean(x * x, axis=-1, keepdims=True)
    return x * lax.rsqrt(ms + EPS) * gain


def _iota(shape, dim):
    return lax.broadcasted_iota(jnp.int32, shape, dim)


def _cparams(sem):
    return pltpu.CompilerParams(dimension_semantics=sem, vmem_limit_bytes=VMEM_LIMIT)


def _const_spec(shape):
    nd = len(shape)
    return pl.BlockSpec(shape, lambda *_: (0,) * nd, pipeline_mode=pl.Buffered(1))


def _mix_kernel(x_ref, g_ref, w_ref, b_ref, wt_ref, bt_ref, poolw_ref, pscale_ref, convw_ref, convb_ref,
                wa_ref, ba_ref, wx_ref, bx_ref, lam_ref, qg_ref, kg_ref, kgc_ref, mqg_ref,
                g64_ref, g128_ref, poolprev_ref, convprev_ref, hprev_ref,
                opool_ref, olru_ref, q_ref, k_ref, kb_ref, v_ref, vb_ref, mq_ref,
                lf_ref, c_ref, poolnew_ref, convnew_ref, hnew_ref,
                d_s, xc_s, a_s, b_s, pool_carry, conv_carry, h_carry, c_carry,
                *, C, H, tm, lb, tiles_per_seq, pos0, att_scale, mem_scale, transposed_kv):
    n_sub = tm // lb
    carry = tiles_per_seq > 1
    i = pl.program_id(0)
    gw = C // len(POOL_WINDOWS)

    hp = w_ref.shape[0] > 1
    act = (lambda v: v) if hp else (lambda v: v.astype(BF16))
    u = _parts(act(_rms(x_ref[...], g_ref[...])))

    def proj(n, width=C):
        cols = slice(n * C, n * C + width)
        return _mm(u, _wparts(w_ref, slice(None), cols)) + b_ref[:, cols]

    if carry:
        seq_off = (i % tiles_per_seq) * tm

        @pl.when(i % tiles_per_seq == 0)
        def _():
            pool_carry[...] = poolprev_ref[0]
            conv_carry[...] = convprev_ref[0]
            h_carry[...] = hprev_ref[0]
            c_carry[...] = jnp.zeros_like(c_carry)
    else:
        seq_off = 0

    row = _iota((lb, 1), 0)
    pos = pos0 + seq_off + row

    zp = proj(0)
    inv_cnt = [1.0 / jnp.minimum(w, pos + 1).astype(F32) for w in POOL_WINDOWS]
    for j in range(n_sub):
        r0 = j * lb
        prev = pool_carry[...] if carry else poolprev_ref[j]
        cur = zp[r0:r0 + lb]
        ext = jnp.concatenate([prev, cur], axis=0)
        for g, w in enumerate(POOL_WINDOWS):
            s = ext[:, g * gw:(g + 1) * gw]
            sh = 1
            while sh < w:
                s = s + pltpu.roll(s, sh, axis=0)
                sh *= 2
            d_s[r0:r0 + lb, g * gw:(g + 1) * gw] = s[POOL_HIST:] * inv_cnt[g] - cur[:, g * gw:(g + 1) * gw]
        new_state = ext[lb:lb + POOL_HIST]
        if carry:
            pool_carry[...] = new_state
            poolnew_ref[0] = new_state
        else:
            poolnew_ref[j] = new_state
    d = act(d_s[...])
    y_pool = jnp.concatenate(
        [_mm(d[:, n * MXU_DIM:(n + 1) * MXU_DIM], _wparts(poolw_ref, n)) for n in range(C // MXU_DIM)], axis=1)
    opool_ref[...] = (y_pool * pscale_ref[...]).astype(opool_ref.dtype)

    zx = proj(1)
    for j in range(n_sub):
        r0 = j * lb
        prev = conv_carry[...] if carry else convprev_ref[j]
        ext = jnp.concatenate([prev, zx[r0:r0 + lb]], axis=0)
        xc = convb_ref[...]
        for t in range(CONV_WIDTH):
            sh = CONV_WIDTH - 1 - t
            e = ext if sh == 0 else pltpu.roll(ext, sh, axis=0)
            xc = xc + e[CONV_HIST:] * convw_ref[t:t + 1, :]
        xc_s[r0:r0 + lb, :] = xc
        new_state = ext[lb:lb + CONV_HIST]
        if carry:
            conv_carry[...] = new_state
            convnew_ref[0] = new_state
        else:
            convnew_ref[j] = new_state

    xc = xc_s[...]
    xcb = act(xc)
    nb = C // MXU_DIM
    ra = jnp.concatenate([_mm(xcb[:, n * MXU_DIM:(n + 1) * MXU_DIM], _wparts(wa_ref, n)) for n in range(nb)], axis=1)
    rx = jnp.concatenate([_mm(xcb[:, n * MXU_DIM:(n + 1) * MXU_DIM], _wparts(wx_ref, n)) for n in range(nb)], axis=1)
    r = jax.nn.sigmoid(ra + ba_ref[...])
    ig = jax.nn.sigmoid(rx + bx_ref[...])
    log_a = (-LRU_C) * r * _softplus(-lam_ref[...])
    a = jnp.exp(log_a)
    th = jnp.tanh(log_a)
    one_minus_a2 = (-2.0) * th / (1.0 - th)
    mult = jnp.sqrt(one_minus_a2)
    pos_tile = pos0 + seq_off + _iota((tm, 1), 0) % lb
    mult = jnp.where(pos_tile == 0, 1.0, mult)
    a_s[...] = a
    b_s[...] = mult * ig * xc

    zg = proj(2)
    for j in range(n_sub):
        r0 = j * lb
        av = a_s[r0:r0 + lb, :]
        bv = b_s[r0:r0 + lb, :]
        sh = 1
        while sh < lb:
            m = row >= sh
            ar = pltpu.roll(av, sh, axis=0)
            br = pltpu.roll(bv, sh, axis=0)
            bv = jnp.where(m, av * br + bv, bv)
            av = jnp.where(m, av * ar, av)
            sh *= 2
        hp = h_carry[...] if carry else hprev_ref[j]
        h = av * hp + bv
        olru_ref[r0:r0 + lb, :] = (h * _gelu_tanh(zg[r0:r0 + lb])).astype(olru_ref.dtype)
        if carry:
            h_carry[...] = h[lb - 1:lb]
            hnew_ref[0] = h[lb - 1:lb]
        else:
            hnew_ref[j] = h[lb - 1:lb]

    zq = proj(3)
    msq = _mm(act(zq * zq), g64_ref[...])
    q_ref[...] = (zq * lax.rsqrt(msq + EPS) * (qg_ref[...] * att_scale)).astype(q_ref.dtype)
    zm = proj(6)
    msm = _mm(act(zm * zm), g128_ref[...])
    mq_ref[...] = (zm * lax.rsqrt(msm + EPS) * (mqg_ref[...] * mem_scale)).astype(mq_ref.dtype)
    zv = proj(5)
    vb_ref[...] = zv.astype(vb_ref.dtype)

    if not transposed_kv:
        zk = proj(4)
        msk = _mm(act(zk * zk), g64_ref[...])
        kn = zk * lax.rsqrt(msk + EPS) * kg_ref[...]
        k_ref[...] = kn
        kb_ref[...] = kn.astype(BF16)
        v_ref[...] = zv
        zf = proj(7, LANES)
        lf = -_softplus(-zf)
        lf_ref[...] = lf[:, :H]
        for j in range(n_sub):
            r0 = j * lb
            cv = lf[r0:r0 + lb]
            sh = 1
            while sh < lb:
                cv = cv + jnp.where(row >= sh, pltpu.roll(cv, sh, axis=0), 0.0)
                sh *= 2
            if carry:
                cv = cv + c_carry[...]
                c_carry[...] = cv[lb - 1:lb]
            c_ref[r0:r0 + lb, :] = cv[:, :H]
    else:
        def proj_t(r0, n):
            return _mm(_wparts(wt_ref, slice(r0, r0 + n), slice(None)), u, "nt") + bt_ref[r0:r0 + n, :]

        zk_t = proj_t(0, C)
        msk = _mm(g64_ref[...], act(zk_t * zk_t))
        kn_t = zk_t * lax.rsqrt(msk + EPS) * kgc_ref[...]
        k_ref[0] = kn_t
        kb_ref[0] = kn_t.astype(BF16)
        v_ref[0] = proj_t(C, C)
        lf_t = -_softplus(-proj_t(2 * C, SUBLANES_BF16))
        lf_ref[0] = lf_t[:H]
        incl = (_iota((tm, tm), 0) <= _iota((tm, tm), 1)).astype(BF16)
        c_t = _dot_exact_rhs01(lf_t, incl)
        if carry:
            c_t = c_t + c_carry[...]
            c_carry[...] = c_carry[...] + jnp.sum(lf_t, axis=1, keepdims=True)
        c_ref[0] = c_t[:H]


def _block_diag(w, size):
    n, a, _ = w.shape
    per = size // a
    w = w.reshape(n // per, per, a, a)
    eye = jnp.eye(per, dtype=w.dtype)
    return jnp.einsum('gpij,pq->gpiqj', w, eye).reshape(n // per, size, size)


def _group_mean_matrix(c, width):
    idx = np.arange(c) // width
    return jnp.asarray((idx[:, None] == idx[None, :]).astype(np.float32) / width, dtype=BF16)


def _mix_call(x, lw, pool_prev, conv_prev, h_prev, *, seq_len, pos0, H, tm, transposed_kv):
    T, D = x.shape
    C = D // 2
    lb = min(tm, seq_len)
    tiles_per_seq = max(seq_len // tm, 1)
    n_sub = tm // lb
    nseq_blk = 1 if tiles_per_seq > 1 else n_sub
    assert T % tm == 0 and tm % lb == 0 and seq_len % lb == 0 and lb % 8 == 0
    assert not transposed_kv or tiles_per_seq > 1 or seq_len == tm
    n_tiles = T // tm
    nseq = pool_prev.shape[0]
    if tiles_per_seq > 1:
        smap = lambda i: (i // tiles_per_seq, 0, 0)
    else:
        smap = lambda i: (i, 0, 0)
    tok = lambda w: pl.BlockSpec((tm, w), lambda i: (i, 0))
    tr = lambda w: pl.BlockSpec((1, w, tm), lambda i: (i // tiles_per_seq, 0, i % tiles_per_seq))
    win = lw['w_a'].shape[2]
    wtr = lw['w_t'].shape[1]
    act_dtype = F32 if lw['w_a'].shape[0] > 1 else BF16
    kern = functools.partial(_mix_kernel, C=C, H=H, tm=tm, lb=lb, tiles_per_seq=tiles_per_seq, pos0=pos0,
                             att_scale=float((C // H) ** -0.5), mem_scale=float(lw['mem_head_dim'] ** -0.5),
                             transposed_kv=transposed_kv)
    in_specs = [tok(D), _const_spec((1, D)), _const_spec(lw['w_a'].shape), _const_spec((1, win)),
                _const_spec(lw['w_t'].shape), _const_spec((wtr, 1)),
                _const_spec(lw['pool_w'].shape), _const_spec((1, C)), _const_spec((CONV_WIDTH, C)),
                _const_spec((1, C)), _const_spec(lw['wa'].shape), _const_spec((1, C)),
                _const_spec(lw['wx'].shape), _const_spec((1, C)), _const_spec((1, C)),
                _const_spec((1, C)), _const_spec((1, C)), _const_spec((C, 1)), _const_spec((1, C)),
                _const_spec((C, C)), _const_spec((C, C)),
                pl.BlockSpec((nseq_blk, POOL_HIST, C), smap), pl.BlockSpec((nseq_blk, CONV_HIST, C), smap),
                pl.BlockSpec((nseq_blk, 1, C), smap)]
    if transposed_kv:
        kv_shape = lambda w, dt: jax.ShapeDtypeStruct((nseq, w, seq_len), dt)
        kv_spec = tr
        c_rows = SUBLANES_BF16
    else:
        kv_shape = lambda w, dt: jax.ShapeDtypeStruct((T, w), dt)
        kv_spec = tok
        c_rows = 1
    out_shape = [jax.ShapeDtypeStruct((T, C), act_dtype), jax.ShapeDtypeStruct((T, C), act_dtype),
                 jax.ShapeDtypeStruct((T, C), act_dtype), kv_shape(C, F32), kv_shape(C, BF16), kv_shape(C, F32),
                 jax.ShapeDtypeStruct((T, C), act_dtype), jax.ShapeDtypeStruct((T, C), act_dtype),
                 kv_shape(H, F32), kv_shape(H, F32),
                 jax.ShapeDtypeStruct((nseq, POOL_HIST, C), F32), jax.ShapeDtypeStruct((nseq, CONV_HIST, C), F32),
                 jax.ShapeDtypeStruct((nseq, 1, C), F32)]
    out_specs = [tok(C), tok(C), tok(C), kv_spec(C), kv_spec(C), kv_spec(C), tok(C), tok(C),
                 kv_spec(H), kv_spec(H),
                 pl.BlockSpec((nseq_blk, POOL_HIST, C), smap),
                 pl.BlockSpec((nseq_blk, CONV_HIST, C), smap),
                 pl.BlockSpec((nseq_blk, 1, C), smap)]
    scratch = [pltpu.VMEM((tm, C), F32)] * 4 + [
        pltpu.VMEM((POOL_HIST, C), F32), pltpu.VMEM((CONV_HIST, C), F32), pltpu.VMEM((1, C), F32),
        pltpu.VMEM((c_rows, 1 if transposed_kv else LANES), F32)]
    return pl.pallas_call(
        kern, grid=(n_tiles,), in_specs=in_specs, out_specs=out_specs, out_shape=out_shape,
        scratch_shapes=scratch, compiler_params=_cparams(("arbitrary",)), name="mix",
    )(x, lw['norm_mix'], lw['w_a'], lw['b_a'], lw['w_t'], lw['b_t'], lw['pool_w'], lw['pool_scale'],
      lw['conv_w'], lw['conv_b'], lw['wa'], lw['ba'], lw['wx'], lw['bx'], lw['lam'], lw['q_gain'],
      lw['k_gain'], lw['k_gain_col'], lw['mq_gain'], lw['g64'], lw['g128'], pool_prev, conv_prev, h_prev)


def _fox_prompt_kernel(q_ref, kt_ref, v_ref, crow_ref, o_ref, qs_s, s_s, p_s, m_s, l_s, al_s, acc_s,
                       *, tq, hg, dh, rc):
    qi = pl.program_id(2)
    width = hg * dh
    n_pieces = qs_s.shape[0]
    lane = _iota((1, width), 1) // dh
    for i, q in enumerate(_parts(q_ref[...])):
        for h in range(hg):
            qs_s[i, h * tq:(h + 1) * tq, :] = jnp.where(lane == h, q, jnp.zeros_like(q))
    m_s[...] = jnp.full_like(m_s, NEG_INF)
    l_s[...] = jnp.zeros_like(l_s)
    acc_s[...] = jnp.zeros_like(acc_s)
    kpos = _iota((1, tq), 1)

    def block(kb, masked):
        start = pl.multiple_of(kb * tq, tq)
        qs = tuple(qs_s[i] for i in range(n_pieces))
        s_s[...] = _mm(qs, kt_ref[0, :, pl.ds(start, tq)])
        crow = crow_ref[0, 0, :, pl.ds(start, tq)]
        for h in range(hg):
            ck = crow[h:h + 1, :]
            for c in range(tq // rc):
                rows = slice(h * tq + c * rc, h * tq + (c + 1) * rc)
                s = s_s[rows, :] - ck
                if masked:
                    s = jnp.where(c * rc + _iota((rc, 1), 0) >= kpos, s, NEG_INF)
                m_old = m_s[rows, :]
                m_new = jnp.maximum(m_old, jnp.max(s, axis=-1, keepdims=True))
                alpha = jnp.exp(m_old - m_new)
                p = jnp.exp(s - m_new)
                l_s[rows, :] = alpha * l_s[rows, :] + jnp.sum(p, axis=-1, keepdims=True)
                m_s[rows, :] = m_new
                al_s[rows, :] = alpha
                for i, piece in enumerate(_parts(p) if n_pieces > 1 else (p.astype(BF16),)):
                    p_s[i, rows, :] = piece
        ps = tuple(p_s[i] for i in range(n_pieces))
        acc_s[...] = acc_s[...] * al_s[...] + _mm(ps, v_ref[pl.ds(start, tq), :])

    def body(kb, c):
        block(kb, False)
        return c

    lax.fori_loop(0, qi, body, 0)
    block(qi, True)
    o = acc_s[...] * (1.0 / l_s[...])
    out = jnp.zeros((tq, width), F32)
    for h in range(hg):
        out = jnp.where(lane == h, o[h * tq:(h + 1) * tq], out)
    o_ref[...] = out.astype(o_ref.dtype)


def _fox_prompt_call(q, kt, v, ct, *, tq):
    T, C = q.shape
    n_pieces = 2 if q.dtype == F32 else 1
    B, H, S = ct.shape
    dh = C // H
    hg = MXU_DIM // dh
    ng = H // hg
    nq = S // tq
    width = hg * dh
    kern = functools.partial(_fox_prompt_kernel, tq=tq, hg=hg, dh=dh, rc=64)
    return pl.pallas_call(
        kern, grid=(B, ng, nq),
        in_specs=[pl.BlockSpec((tq, width), lambda b, g, i: (b * nq + i, g)),
                  pl.BlockSpec((1, width, S), lambda b, g, i: (b, g, 0)),
                  pl.BlockSpec((S, width), lambda b, g, i: (b, g)),
                  pl.BlockSpec((1, 1, hg, S), lambda b, g, i: (b, g, 0, 0))],
        out_specs=pl.BlockSpec((tq, width), lambda b, g, i: (b * nq + i, g)),
        out_shape=jax.ShapeDtypeStruct((T, C), q.dtype),
        scratch_shapes=[pltpu.VMEM((n_pieces, hg * tq, width), BF16), pltpu.VMEM((hg * tq, tq), F32),
                        pltpu.VMEM((n_pieces, hg * tq, tq), BF16), pltpu.VMEM((hg * tq, 1), F32),
                        pltpu.VMEM((hg * tq, 1), F32), pltpu.VMEM((hg * tq, 1), F32),
                        pltpu.VMEM((hg * tq, width), F32)],
        compiler_params=_cparams(("arbitrary", "arbitrary", "arbitrary")), name="fox_prompt",
    )(q, kt, v, ct.reshape(B, ng, hg, S))


def _fox_decode_kernel(pt_ref, q_ref, kn_ref, vn_ref, cn_ref, *rest, pp, H, dh, L, page):
    kt_refs = rest[:pp]
    vt_refs = rest[pp:2 * pp]
    lf_refs = rest[2 * pp:3 * pp]
    o_ref = rest[3 * pp]
    kb_s, vb_s, knp_s, vnp_s, m_s, l_s, acc_s, dsum_s = rest[3 * pp + 1:]
    j = pl.program_id(1)
    nj = pl.num_programs(1)
    C = H * dh
    R = L * H

    q = q_ref[0].astype(F32)
    head_mask = (_iota((1, C), 1) // dh) == _iota((H, 1), 0)
    qs = jnp.concatenate([jnp.where(head_mask, q[t:t + 1, :], 0.0) for t in range(L)], axis=0).astype(BF16)

    @pl.when(j == 0)
    def _():
        knp_s[...] = jnp.zeros_like(knp_s)
        vnp_s[...] = jnp.zeros_like(vnp_s)
        knp_s[0:L, :] = kn_ref[0]
        vnp_s[0:L, :] = vn_ref[0]
        s = _dot_nt(qs, knp_s[...].astype(BF16)) - cn_ref[0]
        s = jnp.where(_iota((R, LANES), 1) <= _iota((R, LANES), 0) // H, s, NEG_INF)
        m_new = jnp.max(s, axis=-1, keepdims=True)
        p = jnp.exp(s - m_new)
        m_s[...] = m_new
        l_s[...] = jnp.sum(p, axis=-1, keepdims=True)
        acc_s[...] = _dot(p.astype(BF16), vnp_s[...].astype(BF16))
        dsum_s[...] = jnp.zeros_like(dsum_s)

    for i in range(pp):
        kb_s[:, i * page:(i + 1) * page] = kt_refs[i][0].astype(BF16)
        vb_s[:, i * page:(i + 1) * page] = vt_refs[i][0].astype(BF16)
    lf_all = jnp.concatenate([lf_refs[i][0] for i in range(pp)], axis=0)
    later = (_iota((page, page), 0) > _iota((page, page), 1)).astype(BF16)
    suffix = _dot_exact_rhs01(lf_all, later)
    total = jnp.sum(lf_all, axis=-1, keepdims=True)
    dsum = dsum_s[...]
    parts = []
    for i in range(pp):
        d_i = suffix[i * H:(i + 1) * H] + dsum
        parts.append(jnp.concatenate([d_i] * L, axis=0))
        dsum = dsum + total[i * H:(i + 1) * H]
    dsum_s[...] = dsum
    d = jnp.concatenate(parts, axis=1) if pp > 1 else parts[0]
    s = _dot(qs, kb_s[...]) + d
    m_old = m_s[...]
    m_new = jnp.maximum(m_old, jnp.max(s, axis=-1, keepdims=True))
    alpha = jnp.exp(m_old - m_new)
    p = jnp.exp(s - m_new)
    l_s[...] = alpha * l_s[...] + jnp.sum(p, axis=-1, keepdims=True)
    m_s[...] = m_new
    acc_s[...] = alpha * acc_s[...] + _dot_nt(p.astype(BF16), vb_s[...])

    @pl.when(j == nj - 1)
    def _():
        o = acc_s[...] * (1.0 / l_s[...])
        keep = (_iota((1, C), 1) // dh) == (_iota((R, 1), 0) % H)
        om = jnp.where(keep, o, 0.0).astype(BF16)
        pick = (_iota((SUBLANES_BF16, R), 1) // H == _iota((SUBLANES_BF16, R), 0)).astype(BF16)
        o_ref[0] = _dot(pick, om)[0:L]


def _fox_decode_call(page_table, q, k_new, v_new, c_new, kt_pages, vt_pages, lf_pages, *, page_base, pp):
    Bs, L, C = q.shape
    H = c_new.shape[-1]
    dh = C // H
    n_pages = page_table.shape[1]
    page = kt_pages.shape[2]
    assert n_pages % pp == 0 and L * H <= LANES and L <= SUBLANES_BF16
    nj = n_pages // pp
    R = L * H
    cn = jnp.broadcast_to(jnp.transpose(c_new, (0, 2, 1))[:, None], (Bs, L, H, L)).reshape(Bs, R, L)
    cn = jnp.pad(cn, ((0, 0), (0, 0), (0, LANES - L)))
    seq = lambda w: pl.BlockSpec((1, L, w), lambda b, j, pt: (b, 0, 0))

    def page_spec(i, rows):
        return pl.BlockSpec((1, rows, page),
                            lambda b, j, pt: (page_base + pt[b, n_pages - 1 - (j * pp + i)], 0, 0))

    in_specs = ([seq(C), seq(C), seq(C), pl.BlockSpec((1, R, LANES), lambda b, j, pt: (b, 0, 0))]
                + [page_spec(i, C) for i in range(pp)] + [page_spec(i, C) for i in range(pp)]
                + [page_spec(i, H) for i in range(pp)])
    kern = functools.partial(_fox_decode_kernel, pp=pp, H=H, dh=dh, L=L, page=page)
    grid_spec = pltpu.PrefetchScalarGridSpec(
        num_scalar_prefetch=1, grid=(Bs, nj), in_specs=in_specs,
        out_specs=pl.BlockSpec((1, L, C), lambda b, j, pt: (b, 0, 0)),
        scratch_shapes=[pltpu.VMEM((C, pp * page), BF16), pltpu.VMEM((C, pp * page), BF16),
                        pltpu.VMEM((LANES, C), F32), pltpu.VMEM((LANES, C), F32),
                        pltpu.VMEM((R, 1), F32), pltpu.VMEM((R, 1), F32),
                        pltpu.VMEM((R, C), F32), pltpu.VMEM((H, 1), F32)])
    return pl.pallas_call(
        kern, grid_spec=grid_spec, out_shape=jax.ShapeDtypeStruct((Bs, L, C), F32),
        compiler_params=_cparams(("arbitrary", "arbitrary")), name="fox_decode",
    )(page_table, q, k_new, v_new, cn, *([kt_pages] * pp), *([vt_pages] * pp), *([lf_pages] * pp))


def _mem_kv_kernel(m_ref, g_ref, w_ref, kg_ref, g128_ref, k_ref, v_ref, kb_ref, vb_ref, *, C):
    hp = w_ref.shape[0] > 1
    act = (lambda v: v) if hp else (lambda v: v.astype(BF16))
    u = _parts(act(_rms(m_ref[...], g_ref[...])))
    zk = _mm(u, _wparts(w_ref, slice(None), slice(0, C)))
    ms = _mm(act(zk * zk), g128_ref[...])
    kn = zk * lax.rsqrt(ms + EPS) * kg_ref[...]
    k_ref[...] = kn
    kb_ref[...] = kn.astype(BF16)
    zv = _mm(u, _wparts(w_ref, slice(None), slice(C, 2 * C)))
    v_ref[...] = zv
    vb_ref[...] = zv.astype(BF16)


def _mem_kv_call(mem, g, w, k_gain, g128, *, tm):
    T, D = mem.shape
    C = w.shape[2] // 2
    tok = lambda wd: pl.BlockSpec((tm, wd), lambda i: (i, 0))
    return pl.pallas_call(
        functools.partial(_mem_kv_kernel, C=C), grid=(T // tm,),
        in_specs=[tok(D), _const_spec((1, D)), _const_spec(w.shape), _const_spec((1, C)), _const_spec((C, C))],
        out_specs=[tok(C)] * 4,
        out_shape=[jax.ShapeDtypeStruct((T, C), F32), jax.ShapeDtypeStruct((T, C), F32),
                   jax.ShapeDtypeStruct((T, C), BF16), jax.ShapeDtypeStruct((T, C), BF16)],
        compiler_params=_cparams(("arbitrary",)), name="mem_kv",
    )(mem, g, w, k_gain, g128)


def _mem_attn_kernel(q_ref, k_ref, v_ref, o_ref, *, heads, dh):
    q = q_ref[0]
    hp = q.dtype == F32
    outs = []
    for h in range(heads):
        sl = slice(h * dh, (h + 1) * dh)
        s = _mm(q[:, sl], k_ref[0, :, sl], "nt")
        p = jnp.exp(s - jnp.max(s, axis=-1, keepdims=True))
        l = jnp.sum(p, axis=-1, keepdims=True)
        outs.append(_mm(p if hp else p.astype(BF16), v_ref[0, :, sl]) * (1.0 / l))
    o_ref[0] = jnp.concatenate(outs, axis=1).astype(o_ref.dtype)


def _mem_attn_call(q, k, v, *, heads, tq):
    B, S, C = q.shape
    M = k.shape[1]
    tq = min(tq, S)
    return pl.pallas_call(
        functools.partial(_mem_attn_kernel, heads=heads, dh=C // heads), grid=(B, S // tq),
        in_specs=[pl.BlockSpec((1, tq, C), lambda b, i: (b, i, 0)),
                  pl.BlockSpec((1, M, C), lambda b, i: (b, 0, 0)),
                  pl.BlockSpec((1, M, C), lambda b, i: (b, 0, 0))],
        out_specs=pl.BlockSpec((1, tq, C), lambda b, i: (b, i, 0)),
        out_shape=jax.ShapeDtypeStruct((B, S, C), q.dtype),
        compiler_params=_cparams(("arbitrary", "arbitrary")), name="mem_attn",
    )(q, k, v)


def _sorted_rows(tm, n_exp):
    worst = 2 * tm + n_exp * (MOE_BLOCK - 1)
    return -(-worst // MXU_DIM) * MXU_DIM


def _pick_matrix(pos_row, rows):
    return jnp.where(_iota((rows, pos_row.shape[1]), 0) == pos_row.astype(jnp.int32), 1.0, 0.0).astype(BF16)


def _merge_kernel(x_ref, op_ref, ol_ref, oa_ref, om_ref, g_ref, wg_ref, bg_ref, wb_ref, wo_ref,
                  gf_ref, wr_ref, br_ref, x1_ref, xs_ref, pos_ref, run_ref,
                  *, D, n_groups, per_group, tm, rows):
    x = x_ref[...]
    hp = wb_ref.shape[0] > 1
    act = (lambda v: v) if hp else (lambda v: v.astype(BF16))
    u = _parts(act(_rms(x, g_ref[...])))
    merged = None
    for n, o_ref in enumerate((op_ref, ol_ref, oa_ref, om_ref)):
        cols = slice(n * D, (n + 1) * D)
        gate = jax.nn.sigmoid(_mm(u, _wparts(wg_ref, slice(None), cols)) + bg_ref[:, cols])
        term = gate * _mm(o_ref[...], _wparts(wb_ref, n))
        merged = term if merged is None else merged + term
    x1 = x + _mm(act(merged), _wparts(wo_ref))
    x1_ref[...] = x1
    t = _rms(x1, gf_ref[...])
    tb = t.astype(BF16)
    t_lo = (t - tb.astype(F32)).astype(BF16)
    w_hi = wr_ref[0]
    w_lo = wr_ref[1]
    logit = _dot_nt(w_hi, tb) + _dot_nt(w_hi, t_lo) + _dot_nt(w_lo, tb) + br_ref[...]
    n_exp = n_groups * per_group
    rid = _iota(logit.shape, 0)
    big = jnp.int32(1 << 20)
    is_g = (rid >= n_exp) & (rid < n_exp + n_groups)
    gl = jnp.where(is_g, logit, NEG_INF)
    gmax = jnp.max(gl, axis=0, keepdims=True)
    gidx = jnp.min(jnp.where(gl == gmax, rid, big), axis=0, keepdims=True) - n_exp
    g_w = 1.0 / jnp.sum(jnp.where(is_g, jnp.exp(gl - gmax), 0.0), axis=0, keepdims=True)
    in_grp = (rid >= gidx * per_group) & (rid < (gidx + 1) * per_group)
    el = jnp.where(in_grp, logit, NEG_INF)
    v1 = jnp.max(el, axis=0, keepdims=True)
    i1 = jnp.min(jnp.where(el == v1, rid, big), axis=0, keepdims=True)
    el2 = jnp.where(rid == i1, NEG_INF, el)
    v2 = jnp.max(el2, axis=0, keepdims=True)
    i2 = jnp.min(jnp.where(el2 == v2, rid, big), axis=0, keepdims=True)
    e21 = jnp.exp(v2 - v1)
    w1 = g_w / (1.0 + e21)
    w2 = g_w * e21 / (1.0 + e21)

    sel1 = rid == i1
    sel2 = rid == i2
    member = jnp.where(sel1 | sel2, 1.0, 0.0)
    count = jnp.sum(member, axis=1, keepdims=True)
    padded = jnp.floor((count + (MOE_BLOCK - 1)) * (1.0 / MOE_BLOCK)) * MOE_BLOCK
    below = jnp.where(_iota((LANES, LANES), 1) < _iota((LANES, LANES), 0), 1.0, 0.0).astype(BF16)
    start = _dot(below, jnp.broadcast_to(padded, (LANES, LANES)).astype(BF16))[:, 0:1]
    earlier = jnp.where(_iota((tm, tm), 0) < _iota((tm, tm), 1), 1.0, 0.0).astype(BF16)
    rank = _dot(member.astype(BF16), earlier)
    where_to = start + rank
    pos1 = jnp.sum(jnp.where(sel1, where_to, 0.0), axis=0, keepdims=True)
    pos2 = jnp.sum(jnp.where(sel2, where_to, 0.0), axis=0, keepdims=True)
    pick1 = _pick_matrix(pos1, rows)
    pick2 = _pick_matrix(pos2, rows)
    xs_ref[:, 0:D] = _dot(pick1 + pick2, tb).astype(BF16)
    r8 = _iota((LANES, tm), 0)

    def pieces(w):
        hi, mid, lo = _split3(w)
        return jnp.where(r8 == 0, hi.astype(F32), jnp.where(r8 == 1, mid.astype(F32),
                         jnp.where(r8 == 2, lo.astype(F32), 0.0))).astype(BF16)

    xs_ref[:, D:D + LANES] = (_dot_nt(pick1, pieces(w1)) + _dot_nt(pick2, pieces(w2))).astype(BF16)
    r8p = _iota((8, tm), 0)
    pos_ref[0] = jnp.where(r8p == 0, pos1, jnp.where(r8p == 1, pos2, 0.0))
    eye = _iota((LANES, LANES), 0) == _iota((LANES, LANES), 1)
    to_row = lambda col: jnp.sum(jnp.where(eye, col, 0.0), axis=0, keepdims=True)
    r8r = _iota((8, LANES), 0)
    run_ref[0] = jnp.where(r8r == 0, to_row(padded), jnp.where(r8r == 1, to_row(start), 0.0))


def _merge_call(x, o_pool, o_lru, o_att, o_mem, lw, *, tm, n_groups, per_group):
    T, D = x.shape
    C = D // 2
    n_tiles = T // tm
    rows = _sorted_rows(tm, n_groups * per_group)
    tok = lambda w: pl.BlockSpec((tm, w), lambda i: (i, 0))
    kern = functools.partial(_merge_kernel, D=D, n_groups=n_groups, per_group=per_group, tm=tm, rows=rows)
    return pl.pallas_call(
        kern, grid=(n_tiles,),
        in_specs=[tok(D), tok(C), tok(C), tok(C), tok(C), _const_spec((1, D)), _const_spec(lw['w_g'].shape),
                  _const_spec((1, 4 * D)), _const_spec(lw['w_branch'].shape), _const_spec(lw['w_out'].shape),
                  _const_spec((1, D)),
                  _const_spec((2, LANES, D)), _const_spec((LANES, 1))],
        out_specs=[tok(D), pl.BlockSpec((rows, D + LANES), lambda i: (i, 0)),
                   pl.BlockSpec((1, 8, tm), lambda i: (i, 0, 0)), pl.BlockSpec((1, 8, LANES), lambda i: (i, 0, 0))],
        out_shape=[jax.ShapeDtypeStruct((T, D), F32), jax.ShapeDtypeStruct((n_tiles * rows, D + LANES), BF16),
                   jax.ShapeDtypeStruct((n_tiles, 8, tm), F32), jax.ShapeDtypeStruct((n_tiles, 8, LANES), F32)],
        compiler_params=_cparams(("arbitrary",)), name="merge",
    )(x, o_pool, o_lru, o_att, o_mem, lw['norm_mix'], lw['w_g'], lw['b_g'], lw['w_branch'], lw['w_out'],
      lw['norm_ffn'], lw['w_route'], lw['b_route'])


def _block_tables(run_info, *, n_exp, rows):
    n_tiles = run_info.shape[0]
    nblk = (run_info[:, 0, :n_exp] / MOE_BLOCK).astype(jnp.int32).T
    first = (run_info[:, 1, :n_exp] / MOE_BLOCK).astype(jnp.int32).T + \
        (jnp.arange(n_tiles, dtype=jnp.int32) * (rows // MOE_BLOCK))[None, :]
    per_expert = jnp.sum(nblk, axis=1)
    chunks_e = (per_expert + MOE_CHUNK_BLOCKS - 1) // MOE_CHUNK_BLOCKS
    chunk_end = jnp.cumsum(chunks_e)
    n_chunks = chunk_end[-1]
    max_blocks = n_tiles * (rows // MOE_BLOCK)
    max_chunks = max_blocks // MOE_CHUNK_BLOCKS + n_exp
    cid = jnp.arange(max_chunks, dtype=jnp.int32)
    chunk_expert = jnp.minimum(jnp.searchsorted(chunk_end, cid, side='right', method='compare_all'),
                               n_exp - 1).astype(jnp.int32)
    last_real = chunk_expert[jnp.maximum(n_chunks - 1, 0)]
    chunk_expert = jnp.where(cid < n_chunks, chunk_expert, last_real)
    slot = jnp.arange(max_chunks * MOE_CHUNK_BLOCKS, dtype=jnp.int32)
    e_of = chunk_expert[slot // MOE_CHUNK_BLOCKS]
    chunk_start = chunk_end - chunks_e
    q = slot - chunk_start[e_of] * MOE_CHUNK_BLOCKS
    valid = (slot // MOE_CHUNK_BLOCKS < n_chunks) & (q < per_expert[e_of])
    flat_n = nblk.reshape(-1)
    flat_end = jnp.cumsum(flat_n)
    expert_base = jnp.cumsum(per_expert) - per_expert
    f = expert_base[e_of] + q
    seg = jnp.minimum(jnp.searchsorted(flat_end, f, side='right', method='compare_all'), flat_n.shape[0] - 1)
    k = f - (flat_end[seg] - flat_n[seg])
    src = jnp.where(valid, first.reshape(-1)[seg] + k, -1).astype(jnp.int32)
    return chunk_expert, src, n_chunks.reshape(1).astype(jnp.int32)


def _expert_kernel(ce_ref, src_ref, nc_ref, xs_in, wgu_ref, wd_ref, xs_out, xbuf, ybuf, in_sem, out_sem,
                   *, D, F):
    del xs_in
    c = pl.program_id(0)
    n = nc_ref[0]
    nb = MOE_CHUNK_BLOCKS
    blk = MOE_BLOCK

    def in_copy(chunk, slot, k):
        row = pl.multiple_of(src_ref[chunk * nb + k] * blk, blk)
        return pltpu.make_async_copy(xs_out.at[pl.ds(row, blk), :], xbuf.at[slot, pl.ds(k * blk, blk), :],
                                     in_sem.at[slot])

    def out_copy(chunk, slot, k):
        row = pl.multiple_of(src_ref[chunk * nb + k] * blk, blk)
        return pltpu.make_async_copy(ybuf.at[slot, pl.ds(k * blk, blk), :],
                                     xs_out.at[pl.ds(row, blk), pl.ds(0, D)], out_sem.at[slot])

    def for_valid(chunk, fn):
        for k in range(nb):
            @pl.when(src_ref[chunk * nb + k] >= 0)
            def _():
                fn(k)

    @pl.when(c == 0)
    def _():
        xbuf[...] = jnp.zeros_like(xbuf)
        for_valid(0, lambda k: in_copy(0, 0, k).start())

    @pl.when(c < n)
    def _():
        slot = c % 2
        for_valid(c, lambda k: in_copy(c, slot, k).wait())

        @pl.when(c + 1 < n)
        def _():
            for_valid(c + 1, lambda k: in_copy(c + 1, 1 - slot, k).start())

        @pl.when(c >= 2)
        def _():
            for_valid(c - 2, lambda k: out_copy(c - 2, slot, k).wait())

        xin = xbuf[slot]
        x = xin[:, 0:D]
        wp = xin[:, D:D + LANES].astype(F32)
        w = wp[:, 0:1] + wp[:, 1:2] + wp[:, 2:3]
        gu = _dot(x, wgu_ref[0])
        h = _silu(gu[:, :F]) * gu[:, F:]
        ybuf[slot] = _dot((h * w).astype(BF16), wd_ref[0]).astype(BF16)
        for_valid(c, lambda k: out_copy(c, slot, k).start())

        @pl.when(c == n - 1)
        def _():
            for_valid(c, lambda k: out_copy(c, slot, k).wait())

            @pl.when(c >= 1)
            def _():
                for_valid(c - 1, lambda k: out_copy(c - 1, 1 - slot, k).wait())


def _expert_call(xs, chunk_expert, src, n_chunks, wgu, wd):
    n_rows, width = xs.shape
    E, D, F2 = wgu.shape
    F = F2 // 2
    max_chunks = chunk_expert.shape[0]
    chunk_rows = MOE_CHUNK_BLOCKS * MOE_BLOCK
    grid_spec = pltpu.PrefetchScalarGridSpec(
        num_scalar_prefetch=3, grid=(max_chunks,),
        in_specs=[pl.BlockSpec(memory_space=pl.ANY),
                  pl.BlockSpec((1, D, F2), lambda c, ce, src, nc: (ce[c], 0, 0)),
                  pl.BlockSpec((1, F, D), lambda c, ce, src, nc: (ce[c], 0, 0))],
        out_specs=pl.BlockSpec(memory_space=pl.ANY),
        scratch_shapes=[pltpu.VMEM((2, chunk_rows, width), BF16), pltpu.VMEM((2, chunk_rows, D), BF16),
                        pltpu.SemaphoreType.DMA((2,)), pltpu.SemaphoreType.DMA((2,))])
    return pl.pallas_call(
        functools.partial(_expert_kernel, D=D, F=F), grid_spec=grid_spec,
        out_shape=jax.ShapeDtypeStruct((n_rows, width), BF16),
        input_output_aliases={3: 0},
        compiler_params=_cparams(("arbitrary",)), name="experts",
    )(chunk_expert, src, n_chunks, xs, wgu, wd)


def _combine_kernel(x1_ref, ys_ref, pos_ref, y_ref, *, D, rows):
    pos = pos_ref[0]
    pick = _pick_matrix(pos[0:1, :], rows) + _pick_matrix(pos[1:2, :], rows)
    y_ref[...] = x1_ref[...] + _dot_tn(pick, ys_ref[:, 0:D])


def _combine_call(x1, ys, pos, *, tm, rows):
    T, D = x1.shape
    width = ys.shape[1]
    return pl.pallas_call(
        functools.partial(_combine_kernel, D=D, rows=rows), grid=(T // tm,),
        in_specs=[pl.BlockSpec((tm, D), lambda i: (i, 0)), pl.BlockSpec((rows, width), lambda i: (i, 0)),
                  pl.BlockSpec((1, 8, tm), lambda i: (i, 0, 0))],
        out_specs=pl.BlockSpec((tm, D), lambda i: (i, 0)),
        out_shape=jax.ShapeDtypeStruct((T, D), F32),
        compiler_params=_cparams(("arbitrary",)), name="combine",
    )(x1, ys, pos)


PIECED_WEIGHTS = ('w_a', 'w_t', 'pool_w', 'wa', 'wx', 'w_mem_kv', 'w_g', 'w_branch', 'w_out')


def _single_pass(lw):
    return {k: (v[:1] if k in PIECED_WEIGHTS else v) for k, v in lw.items()}


def _prep_layer(l, p, *, C, H, mem_heads, high_precision):
    D = 2 * C
    pieces = functools.partial(_stack_parts, high_precision=high_precision)
    w_in, b_in = p['w_in'][l], p['b_in'][l]
    f0 = 6 * C
    w_a = pieces(jnp.concatenate([w_in[:, :f0], w_in[:, f0 + H:f0 + H + C], w_in[:, f0:f0 + H],
                                  jnp.zeros((D, LANES - H), F32)], axis=1))
    b_a = jnp.concatenate([b_in[:f0], b_in[f0 + H:f0 + H + C], b_in[f0:f0 + H], jnp.zeros((LANES - H,), F32)])[None]
    w_t = pieces(jnp.concatenate([w_in[:, 4 * C:6 * C].T, w_in[:, f0:f0 + H].T,
                                  jnp.zeros((SUBLANES_BF16 - H, D), F32)], axis=0))
    b_t = jnp.concatenate([b_in[4 * C:6 * C], b_in[f0:f0 + H], jnp.zeros((SUBLANES_BF16 - H,), F32)])[:, None]
    g0 = f0 + H + C
    n_exp = p['w_router'].shape[2]
    n_grp = p['w_group'].shape[2]
    w_r = jnp.concatenate([p['w_router'][l].T, p['w_group'][l].T,
                           jnp.zeros((LANES - n_exp - n_grp, D), F32)], axis=0)
    w_r_hi, w_r_lo = _stack_parts(w_r, True)
    b_r = jnp.concatenate([p['b_router'][l], p['b_group'][l], jnp.zeros((LANES - n_exp - n_grp,), F32)])[:, None]
    wg = p['w_gate'][l].reshape((n_exp,) + p['w_gate'].shape[3:])
    wu = p['w_up'][l].reshape((n_exp,) + p['w_up'].shape[3:])
    wd = p['w_down'][l].reshape((n_exp,) + p['w_down'].shape[3:])
    dh = C // H
    mdh = C // mem_heads
    k_gain = jnp.tile(p['k_norm'][l], H)
    return dict(
        mem_head_dim=mdh,
        norm_mix=p['norm_mix'][l][None], w_a=w_a, b_a=b_a, w_t=w_t, b_t=b_t,
        pool_w=pieces(_block_diag(p['pool_w'][l], MXU_DIM)), pool_scale=p['pool_scale'][l][None],
        conv_w=p['conv_w'][l], conv_b=p['conv_b'][l][None],
        wa=pieces(_block_diag(p['lru_wa'][l], MXU_DIM)), ba=p['lru_ba'][l][None],
        wx=pieces(_block_diag(p['lru_wx'][l], MXU_DIM)), bx=p['lru_bx'][l][None],
        lam=p['lru_lambda'][l][None],
        q_gain=jnp.tile(p['q_norm'][l], H)[None], k_gain=k_gain[None], k_gain_col=k_gain[:, None],
        mq_gain=jnp.tile(p['mq_norm'][l], mem_heads)[None], mk_gain=jnp.tile(p['mk_norm'][l], mem_heads)[None],
        g64=_group_mean_matrix(C, dh), g128=_group_mean_matrix(C, mdh),
        mem_norm=p['mem_norm'][l][None], w_mem_kv=pieces(p['w_mem_kv'][l]),
        w_g=pieces(w_in[:, g0:]), b_g=b_in[g0:][None],
        w_branch=pieces(p['w_branch'][l]), w_out=pieces(p['w_out'][l]),
        norm_ffn=p['norm_ffn'][l][None], w_route=jnp.stack([w_r_hi, w_r_lo]), b_route=b_r,
        w_gu=jnp.concatenate([wg, wu], axis=2).astype(BF16), w_d=wd.astype(BF16),
        n_groups=n_grp, per_group=n_exp // n_grp,
    )


def _tail(x, o_pool, o_lru, o_att, o_mem, lw, *, tm):
    n_exp = lw['n_groups'] * lw['per_group']
    rows = _sorted_rows(tm, n_exp)
    x1, xs, pos, run_info = _merge_call(x, o_pool, o_lru, o_att, o_mem, lw, tm=tm,
                                        n_groups=lw['n_groups'], per_group=lw['per_group'])
    chunk_expert, src, n_chunks = _block_tables(run_info, n_exp=n_exp, rows=rows)
    ys = _expert_call(xs, chunk_expert, src, n_chunks, lw['w_gu'], lw['w_d'])
    return _combine_call(x1, ys, pos, tm=tm, rows=rows)


def kernel(x_prompt, x_sample, cache_k, cache_v, cache_logf, cache_mem_k, cache_mem_v, state_pool, state_conv,
           state_lru, page_table, mem_prompt, norm_mix, w_in, b_in, pool_w, pool_scale, conv_w, conv_b, lru_wa,
           lru_ba, lru_wx, lru_bx, lru_lambda, q_norm, k_norm, mq_norm, mem_norm, w_mem_kv, mk_norm, w_branch,
           w_out, norm_ffn, w_group, b_group, w_router, b_router, w_gate, w_up, w_down):
    p = dict(norm_mix=norm_mix, w_in=w_in, b_in=b_in, pool_w=pool_w, pool_scale=pool_scale, conv_w=conv_w,
             conv_b=conv_b, lru_wa=lru_wa, lru_ba=lru_ba, lru_wx=lru_wx, lru_bx=lru_bx, lru_lambda=lru_lambda,
             q_norm=q_norm, k_norm=k_norm, mq_norm=mq_norm, mem_norm=mem_norm, w_mem_kv=w_mem_kv, mk_norm=mk_norm,
             w_branch=w_branch, w_out=w_out, norm_ffn=norm_ffn, w_group=w_group, b_group=b_group,
             w_router=w_router, b_router=b_router, w_gate=w_gate, w_up=w_up, w_down=w_down)
    depth = w_in.shape[0]
    B, S, D = x_prompt.shape
    Bs, L, _ = x_sample.shape
    C = D // 2
    _, n_pool, page, H, dh = cache_k.shape
    M, mem_heads, mdh = cache_mem_k.shape[2:]
    n_pages = page_table.shape[1]
    past_len = n_pages * page
    n_pool_rows = state_pool.shape[2]
    n_conv_rows = state_conv.shape[2]
    Tp, Ts = B * S, Bs * L
    tm = TOKEN_TILE
    tm_s = min(tm, Ts)

    xp = x_prompt.reshape(Tp, D)
    xs = x_sample.reshape(Ts, D)
    memp = mem_prompt.reshape(B * M, D)
    pool0 = jnp.zeros((B, POOL_HIST, C), F32)
    conv0 = jnp.zeros((B, CONV_HIST, C), F32)
    h0 = jnp.zeros((B, 1, C), F32)
    kt_pages = jnp.transpose(cache_k, (0, 1, 3, 4, 2)).reshape(depth * n_pool, C, page)
    vt_pages = jnp.transpose(cache_v, (0, 1, 3, 4, 2)).reshape(depth * n_pool, C, page)
    lf_pages = jnp.transpose(cache_logf, (0, 1, 3, 2)).reshape(depth * n_pool, H, page)

    outs = {n: [] for n in ('kp', 'vp', 'lfp', 'mkp', 'mvp', 'poolp', 'convp', 'lrup',
                            'ks', 'vs', 'lfs', 'pools', 'convs', 'lrus')}
    for l in range(depth):
        hp = l < depth - 1
        lw = _prep_layer(l, p, C=C, H=H, mem_heads=mem_heads, high_precision=hp)
        lw_s = _single_pass(lw)
        mk, mv, mkb, mvb = _mem_kv_call(memp, lw['mem_norm'], lw['w_mem_kv'], lw['mk_gain'], lw['g128'], tm=512)
        (o_pool, o_lru, q, kt, ktb, vt, vb, mq, lft, ct, pool_n, conv_n, h_n) = _mix_call(
            xp, lw, pool0, conv0, h0, seq_len=S, pos0=0, H=H, tm=tm, transposed_kv=True)
        o_att = _fox_prompt_call(q, kt if hp else ktb, vb, ct, tq=256)
        o_mem = _mem_attn_call(mq.reshape(B, S, C), (mk if hp else mkb).reshape(B, M, C),
                               (mv if hp else mvb).reshape(B, M, C), heads=mem_heads, tq=512).reshape(Tp, C)
        xp = _tail(xp, o_pool, o_lru, o_att, o_mem, lw, tm=tm)
        outs['kp'].append(jnp.transpose(kt.reshape(B, H, dh, S), (0, 3, 1, 2)))
        outs['vp'].append(jnp.transpose(vt.reshape(B, H, dh, S), (0, 3, 1, 2)))
        outs['lfp'].append(jnp.transpose(lft, (0, 2, 1)))
        outs['mkp'].append(mk.reshape(B, M, mem_heads, mdh)); outs['mvp'].append(mv.reshape(B, M, mem_heads, mdh))
        outs['poolp'].append(pool_n[:, POOL_HIST - n_pool_rows:]); outs['convp'].append(conv_n[:, CONV_HIST - n_conv_rows:])
        outs['lrup'].append(h_n[:, 0])
        pool_prev = jnp.pad(state_pool[l], ((0, 0), (POOL_HIST - n_pool_rows, 0), (0, 0)))
        conv_prev = jnp.pad(state_conv[l], ((0, 0), (CONV_HIST - n_conv_rows, 0), (0, 0)))
        (o_pool, o_lru, q, k, kb, v, vb, mq, lf, c, pool_n, conv_n, h_n) = _mix_call(
            xs, lw_s, pool_prev, conv_prev, state_lru[l][:, None], seq_len=L, pos0=past_len, H=H, tm=tm_s,
            transposed_kv=False)
        o_att = _fox_decode_call(page_table, q.reshape(Bs, L, C), k.reshape(Bs, L, C), v.reshape(Bs, L, C),
                                 c.reshape(Bs, L, H), kt_pages, vt_pages, lf_pages,
                                 page_base=l * n_pool, pp=8).reshape(Ts, C).astype(BF16)
        o_mem = _mem_attn_call(mq.reshape(Bs, L, C), cache_mem_k[l].reshape(Bs, M, C).astype(BF16),
                               cache_mem_v[l].reshape(Bs, M, C).astype(BF16), heads=mem_heads, tq=L).reshape(Ts, C)
        xs = _tail(xs, o_pool, o_lru, o_att, o_mem, lw_s, tm=tm_s)
        outs['ks'].append(k.reshape(Bs, L, H, dh)); outs['vs'].append(v.reshape(Bs, L, H, dh))
        outs['lfs'].append(lf.reshape(Bs, L, H))
        outs['pools'].append(pool_n[:, POOL_HIST - n_pool_rows:]); outs['convs'].append(conv_n[:, CONV_HIST - n_conv_rows:])
        outs['lrus'].append(h_n[:, 0])

    st = lambda n: jnp.stack(outs[n])
    return (xp.reshape(B, S, D), xs.reshape(Bs, L, D),
            st('kp'), st('vp'), st('lfp'), st('mkp'), st('mvp'), st('poolp'), st('convp'), st('lrup'),
            st('ks'), st('vs'), st('lfs'), st('pools'), st('convs'), st('lrus'))
```

```python
import functools
import math

import jax
import jax.numpy as jnp
import numpy as np
from jax import lax
from jax.experimental import pallas as pl
from jax.experimental.pallas import tpu as pltpu

F32 = jnp.float32
BF16 = jnp.bfloat16

EPS = 1e-6
LRU_C = 8.0
POOL_WINDOWS = (2, 4, 8, 16)
POOL_HIST = 16
CONV_WIDTH = 4
CONV_HIST = 8
LANES = 128
SUBLANES_BF16 = 16
MXU_DIM = 256
VMEM_LIMIT = 56 * 1024 * 1024
TOKEN_TILE = 256
MOE_BLOCK = SUBLANES_BF16
MOE_CHUNK_BLOCKS = 16

NEG_INF = float("-inf")


def _dot(a, b):
    return jnp.dot(a, b, preferred_element_type=F32)


def _dot_nt(a, b):
    return lax.dot_general(a, b, (((1,), (1,)), ((), ())), preferred_element_type=F32)


def _dot_tn(a, b):
    return lax.dot_general(a, b, (((0,), (0,)), ((), ())), preferred_element_type=F32)


def _split3(x):
    hi = x.astype(BF16)
    r1 = x - hi.astype(F32)
    mid = r1.astype(BF16)
    lo = (r1 - mid.astype(F32)).astype(BF16)
    return hi, mid, lo


def _dot_exact_rhs01(x, m01):
    hi, mid, lo = _split3(x)
    return _dot(hi, m01) + _dot(mid, m01) + _dot(lo, m01)


def _parts(x):
    if isinstance(x, tuple):
        return x
    if x.dtype == BF16:
        return (x,)
    hi = x.astype(BF16)
    return hi, (x - hi.astype(F32)).astype(BF16)


def _mm(a, b, kind="nn"):
    pa, pb = _parts(a), _parts(b)
    f = {"nn": _dot, "nt": _dot_nt, "tn": _dot_tn}[kind]
    out = f(pa[0], pb[0])
    if len(pa) > 1:
        out = out + f(pa[1], pb[0])
    if len(pb) > 1:
        out = out + f(pa[0], pb[1])
    return out


def _wparts(ref, *idx):
    return tuple(ref[(p,) + idx] for p in range(ref.shape[0]))


def _round_to_bf16(w):
    return lax.reduce_precision(w, exponent_bits=8, mantissa_bits=7)


def _stack_parts(w, high_precision):
    hi = _round_to_bf16(w)
    if not high_precision:
        return hi.astype(BF16)[None]
    return jnp.stack([hi.astype(BF16), (w - hi).astype(BF16)])


def _softplus(y):
    return jnp.maximum(y, 0.0) + jnp.log1p(jnp.exp(-jnp.abs(y)))


def _gelu_tanh(x):
    return x * (0.5 * (1.0 + jnp.tanh(math.sqrt(2.0 / math.pi) * (x + 0.044715 * (x * x * x)))))


def _silu(x):
    return x * jax.nn.sigmoid(x)


def _rms(x, gain):
    ms = jnp.mean(x * x, axis=-1, keepdims=True)
    return x * lax.rsqrt(ms + EPS) * gain


def _iota(shape, dim):
    return lax.broadcasted_iota(jnp.int32, shape, dim)


def _cparams(sem):
    return pltpu.CompilerParams(dimension_semantics=sem, vmem_limit_bytes=VMEM_LIMIT)


def _const_spec(shape):
    nd = len(shape)
    return pl.BlockSpec(shape, lambda *_: (0,) * nd, pipeline_mode=pl.Buffered(1))


def _mix_kernel(x_ref, g_ref, w_ref, b_ref, wt_ref, bt_ref, poolw_ref, pscale_ref, convw_ref, convb_ref,
                wa_ref, ba_ref, wx_ref, bx_ref, lam_ref, qg_ref, qgc_ref, kg_ref, kgc_ref, mqg_ref,
                g64_ref, g128_ref, poolprev_ref, convprev_ref, hprev_ref,
                opool_ref, olru_ref, mq_ref, q_ref, k_ref, v_ref, lf_ref, c_ref, *rest,
                C, H, tm, lb, tiles_per_seq, pos0, att_scale, mem_scale, transposed_kv):
    if transposed_kv:
        krow_ref, *rest = rest
    (poolnew_ref, convnew_ref, hnew_ref, d_s, xc_s, a_s, b_s, pool_carry, conv_carry, h_carry, c_carry) = rest
    n_sub = tm // lb
    carry = tiles_per_seq > 1
    i = pl.program_id(0)
    gw = C // len(POOL_WINDOWS)

    hp = w_ref.shape[0] > 1
    act = (lambda v: v) if hp else (lambda v: v.astype(BF16))
    u = _parts(act(_rms(x_ref[...], g_ref[...])))

    def proj(n, width=C):
        cols = slice(n * C, n * C + width)
        return _mm(u, _wparts(w_ref, slice(None), cols)) + b_ref[:, cols]

    if carry:
        seq_off = (i % tiles_per_seq) * tm

        @pl.when(i % tiles_per_seq == 0)
        def _():
            pool_carry[...] = poolprev_ref[0]
            conv_carry[...] = convprev_ref[0]
            h_carry[...] = hprev_ref[0]
            c_carry[...] = jnp.zeros_like(c_carry)
    else:
        seq_off = 0

    row = _iota((lb, 1), 0)
    pos = pos0 + seq_off + row

    zp = proj(0)
    inv_cnt = [1.0 / jnp.minimum(w, pos + 1).astype(F32) for w in POOL_WINDOWS]
    for j in range(n_sub):
        r0 = j * lb
        prev = pool_carry[...] if carry else poolprev_ref[j]
        cur = zp[r0:r0 + lb]
        ext = jnp.concatenate([prev, cur], axis=0)
        for g, w in enumerate(POOL_WINDOWS):
            s = ext[:, g * gw:(g + 1) * gw]
            sh = 1
            while sh < w:
                s = s + pltpu.roll(s, sh, axis=0)
                sh *= 2
            d_s[r0:r0 + lb, g * gw:(g + 1) * gw] = s[POOL_HIST:] * inv_cnt[g] - cur[:, g * gw:(g + 1) * gw]
        new_state = ext[lb:lb + POOL_HIST]
        if carry:
            pool_carry[...] = new_state
            poolnew_ref[0] = new_state
        else:
            poolnew_ref[j] = new_state
    d = act(d_s[...])
    y_pool = jnp.concatenate(
        [_mm(d[:, n * MXU_DIM:(n + 1) * MXU_DIM], _wparts(poolw_ref, n)) for n in range(C // MXU_DIM)], axis=1)
    opool_ref[...] = (y_pool * pscale_ref[...]).astype(opool_ref.dtype)

    zx = proj(1)
    for j in range(n_sub):
        r0 = j * lb
        prev = conv_carry[...] if carry else convprev_ref[j]
        ext = jnp.concatenate([prev, zx[r0:r0 + lb]], axis=0)
        xc = convb_ref[...]
        for t in range(CONV_WIDTH):
            sh = CONV_WIDTH - 1 - t
            e = ext if sh == 0 else pltpu.roll(ext, sh, axis=0)
            xc = xc + e[CONV_HIST:] * convw_ref[t:t + 1, :]
        xc_s[r0:r0 + lb, :] = xc
        new_state = ext[lb:lb + CONV_HIST]
        if carry:
            conv_carry[...] = new_state
            convnew_ref[0] = new_state
        else:
            convnew_ref[j] = new_state

    xc = xc_s[...]
    xcb = act(xc)
    nb = C // MXU_DIM
    ra = jnp.concatenate([_mm(xcb[:, n * MXU_DIM:(n + 1) * MXU_DIM], _wparts(wa_ref, n)) for n in range(nb)], axis=1)
    rx = jnp.concatenate([_mm(xcb[:, n * MXU_DIM:(n + 1) * MXU_DIM], _wparts(wx_ref, n)) for n in range(nb)], axis=1)
    r = jax.nn.sigmoid(ra + ba_ref[...])
    ig = jax.nn.sigmoid(rx + bx_ref[...])
    log_a = (-LRU_C) * r * _softplus(-lam_ref[...])
    a = jnp.exp(log_a)
    th = jnp.tanh(log_a)
    one_minus_a2 = (-2.0) * th / (1.0 - th)
    mult = jnp.sqrt(one_minus_a2)
    pos_tile = pos0 + seq_off + _iota((tm, 1), 0) % lb
    mult = jnp.where(pos_tile == 0, 1.0, mult)
    a_s[...] = a
    b_s[...] = mult * ig * xc

    zg = proj(2)
    for j in range(n_sub):
        r0 = j * lb
        av = a_s[r0:r0 + lb, :]
        bv = b_s[r0:r0 + lb, :]
        sh = 1
        while sh < lb:
            m = row >= sh
            ar = pltpu.roll(av, sh, axis=0)
            br = pltpu.roll(bv, sh, axis=0)
            bv = jnp.where(m, av * br + bv, bv)
            av = jnp.where(m, av * ar, av)
            sh *= 2
        hp = h_carry[...] if carry else hprev_ref[j]
        h = av * hp + bv
        olru_ref[r0:r0 + lb, :] = (h * _gelu_tanh(zg[r0:r0 + lb])).astype(olru_ref.dtype)
        if carry:
            h_carry[...] = h[lb - 1:lb]
            hnew_ref[0] = h[lb - 1:lb]
        else:
            hnew_ref[j] = h[lb - 1:lb]

    zm = proj(6)
    msm = _mm(act(zm * zm), g128_ref[...])
    mq_ref[...] = (zm * lax.rsqrt(msm + EPS) * (mqg_ref[...] * mem_scale)).astype(mq_ref.dtype)
    zk = proj(4)
    msk = _mm(act(zk * zk), g64_ref[...])
    kn = zk * lax.rsqrt(msk + EPS) * kg_ref[...]

    if not transposed_kv:
        zq = proj(3)
        msq = _mm(act(zq * zq), g64_ref[...])
        q_ref[...] = (zq * lax.rsqrt(msq + EPS) * (qg_ref[...] * att_scale)).astype(q_ref.dtype)
        k_ref[...] = kn
        v_ref[...] = proj(5)
        zf = proj(7, LANES)
        lf = -_softplus(-zf)
        lf_ref[...] = lf[:, :H]
        for j in range(n_sub):
            r0 = j * lb
            cv = lf[r0:r0 + lb]
            sh = 1
            while sh < lb:
                cv = cv + jnp.where(row >= sh, pltpu.roll(cv, sh, axis=0), 0.0)
                sh *= 2
            if carry:
                cv = cv + c_carry[...]
                c_carry[...] = cv[lb - 1:lb]
            c_ref[r0:r0 + lb, :] = cv[:, :H]
    else:
        krow_ref[...] = kn.astype(krow_ref.dtype)

        def proj_t(r0, n):
            return _mm(_wparts(wt_ref, slice(r0, r0 + n), slice(None)), u, "nt") + bt_ref[r0:r0 + n, :]

        zq_t = proj_t(0, C)
        msq = _mm(g64_ref[...], act(zq_t * zq_t))
        q_ref[0] = (zq_t * lax.rsqrt(msq + EPS) * (qgc_ref[...] * att_scale)).astype(q_ref.dtype)
        zk_t = proj_t(C, C)
        msk_t = _mm(g64_ref[...], act(zk_t * zk_t))
        k_ref[0] = zk_t * lax.rsqrt(msk_t + EPS) * kgc_ref[...]
        v_ref[0] = proj_t(2 * C, C)
        lf_t = -_softplus(-proj_t(3 * C, SUBLANES_BF16))
        lf_ref[0] = lf_t[:H]
        incl = (_iota((tm, tm), 0) <= _iota((tm, tm), 1)).astype(BF16)
        c_t = _dot_exact_rhs01(lf_t, incl)
        if carry:
            c_t = c_t + c_carry[...]
            c_carry[...] = c_carry[...] + jnp.sum(lf_t, axis=1, keepdims=True)
        c_ref[0] = c_t[:H]


def _block_diag(w, size):
    n, a, _ = w.shape
    per = size // a
    w = w.reshape(n // per, per, a, a)
    eye = jnp.eye(per, dtype=w.dtype)
    return jnp.einsum('gpij,pq->gpiqj', w, eye).reshape(n // per, size, size)


def _group_mean_matrix(c, width):
    idx = np.arange(c) // width
    return jnp.asarray((idx[:, None] == idx[None, :]).astype(np.float32) / width, dtype=BF16)


def _mix_call(x, lw, pool_prev, conv_prev, h_prev, *, seq_len, pos0, H, tm, transposed_kv):
    T, D = x.shape
    C = D // 2
    lb = min(tm, seq_len)
    tiles_per_seq = max(seq_len // tm, 1)
    n_sub = tm // lb
    nseq_blk = 1 if tiles_per_seq > 1 else n_sub
    assert T % tm == 0 and tm % lb == 0 and seq_len % lb == 0 and lb % 8 == 0
    assert not transposed_kv or tiles_per_seq > 1 or seq_len == tm
    n_tiles = T // tm
    nseq = pool_prev.shape[0]
    if tiles_per_seq > 1:
        smap = lambda i: (i // tiles_per_seq, 0, 0)
    else:
        smap = lambda i: (i, 0, 0)
    tok = lambda w: pl.BlockSpec((tm, w), lambda i: (i, 0))
    tr = lambda w: pl.BlockSpec((1, w, tm), lambda i: (i // tiles_per_seq, 0, i % tiles_per_seq))
    win = lw['w_a'].shape[2]
    wtr = lw['w_t'].shape[1]
    act_dtype = F32 if lw['w_a'].shape[0] > 1 else BF16
    kern = functools.partial(_mix_kernel, C=C, H=H, tm=tm, lb=lb, tiles_per_seq=tiles_per_seq, pos0=pos0,
                             att_scale=float((C // H) ** -0.5), mem_scale=float(lw['mem_head_dim'] ** -0.5),
                             transposed_kv=transposed_kv)
    in_specs = [tok(D), _const_spec((1, D)), _const_spec(lw['w_a'].shape), _const_spec((1, win)),
                _const_spec(lw['w_t'].shape), _const_spec((wtr, 1)),
                _const_spec(lw['pool_w'].shape), _const_spec((1, C)), _const_spec((CONV_WIDTH, C)),
                _const_spec((1, C)), _const_spec(lw['wa'].shape), _const_spec((1, C)),
                _const_spec(lw['wx'].shape), _const_spec((1, C)), _const_spec((1, C)),
                _const_spec((1, C)), _const_spec((C, 1)), _const_spec((1, C)), _const_spec((C, 1)),
                _const_spec((1, C)), _const_spec((C, C)), _const_spec((C, C)),
                pl.BlockSpec((nseq_blk, POOL_HIST, C), smap), pl.BlockSpec((nseq_blk, CONV_HIST, C), smap),
                pl.BlockSpec((nseq_blk, 1, C), smap)]
    if transposed_kv:
        kv_shape = lambda w, dt: jax.ShapeDtypeStruct((nseq, w, seq_len), dt)
        kv_spec = tr
        c_rows = SUBLANES_BF16
    else:
        kv_shape = lambda w, dt: jax.ShapeDtypeStruct((T, w), dt)
        kv_spec = tok
        c_rows = 1
    out_shape = [jax.ShapeDtypeStruct((T, C), act_dtype)] * 3 + [
        kv_shape(C, act_dtype), kv_shape(C, F32), kv_shape(C, F32), kv_shape(H, F32), kv_shape(H, F32)]
    out_specs = [tok(C)] * 3 + [kv_spec(C), kv_spec(C), kv_spec(C), kv_spec(H), kv_spec(H)]
    if transposed_kv:
        out_shape.append(jax.ShapeDtypeStruct((T, C), act_dtype))
        out_specs.append(tok(C))
    out_shape += [jax.ShapeDtypeStruct((nseq, POOL_HIST, C), F32), jax.ShapeDtypeStruct((nseq, CONV_HIST, C), F32),
                  jax.ShapeDtypeStruct((nseq, 1, C), F32)]
    out_specs += [pl.BlockSpec((nseq_blk, POOL_HIST, C), smap), pl.BlockSpec((nseq_blk, CONV_HIST, C), smap),
                  pl.BlockSpec((nseq_blk, 1, C), smap)]
    scratch = [pltpu.VMEM((tm, C), F32)] * 4 + [
        pltpu.VMEM((POOL_HIST, C), F32), pltpu.VMEM((CONV_HIST, C), F32), pltpu.VMEM((1, C), F32),
        pltpu.VMEM((c_rows, 1 if transposed_kv else LANES), F32)]
    return pl.pallas_call(
        kern, grid=(n_tiles,), in_specs=in_specs, out_specs=out_specs, out_shape=out_shape,
        scratch_shapes=scratch, compiler_params=_cparams(("arbitrary",)), name="mix",
    )(x, lw['norm_mix'], lw['w_a'], lw['b_a'], lw['w_t'], lw['b_t'], lw['pool_w'], lw['pool_scale'],
      lw['conv_w'], lw['conv_b'], lw['wa'], lw['ba'], lw['wx'], lw['bx'], lw['lam'], lw['q_gain'],
      lw['q_gain_col'], lw['k_gain'], lw['k_gain_col'], lw['mq_gain'], lw['g64'], lw['g128'],
      pool_prev, conv_prev, h_prev)


def _fox_prompt_kernel(qt_ref, k_ref, vt_ref, ccol_ref, o_ref, qs_s, s_s, p_s, m_s, l_s, acc_s, *, tq, hg, dh):
    qi = pl.program_id(2)
    width = hg * dh
    n_pieces = qs_s.shape[0]
    row_h = _iota((width, 1), 0) // dh
    for i, q in enumerate(_parts(qt_ref[0])):
        for h in range(hg):
            qs_s[i, :, h * tq:(h + 1) * tq] = jnp.where(row_h == h, q, jnp.zeros_like(q))
    m_s[...] = jnp.full_like(m_s, NEG_INF)
    l_s[...] = jnp.zeros_like(l_s)
    acc_s[...] = jnp.zeros_like(acc_s)
    kpos = _iota((tq, 1), 0)
    operand = (lambda v: v) if n_pieces > 1 else (lambda v: v.astype(BF16))

    def block(kb, masked):
        start = pl.multiple_of(kb * tq, tq)
        qs = tuple(qs_s[i] for i in range(n_pieces))
        s_s[...] = _mm(k_ref[pl.ds(start, tq), :], qs)
        cc = ccol_ref[0, 0, pl.ds(start, tq), :]
        for h in range(hg):
            bias = jnp.broadcast_to(cc[:, h:h + 1], (tq, LANES))
            for j in range(tq // LANES):
                lanes = slice(h * tq + j * LANES, h * tq + (j + 1) * LANES)
                s = s_s[:, lanes] - bias
                if masked:
                    s = jnp.where(j * LANES + _iota((1, LANES), 1) >= kpos, s, NEG_INF)
                m_old = m_s[:, lanes]
                m_new = jnp.maximum(m_old, jnp.max(s, axis=0, keepdims=True))
                alpha = jnp.exp(m_old - m_new)
                p = jnp.exp(s - m_new)
                l_s[:, lanes] = alpha * l_s[:, lanes] + jnp.sum(p, axis=0, keepdims=True)
                m_s[:, lanes] = m_new
                acc_rows = slice(h * dh, (h + 1) * dh)
                acc_lanes = slice(j * LANES, (j + 1) * LANES)
                acc_s[acc_rows, acc_lanes] = acc_s[acc_rows, acc_lanes] * alpha
                for i, piece in enumerate(_parts(operand(p))):
                    p_s[i, :, lanes] = piece
        v = operand(vt_ref[0, :, pl.ds(start, tq)])
        for h in range(hg):
            ps = tuple(p_s[i, :, h * tq:(h + 1) * tq] for i in range(n_pieces))
            acc_s[h * dh:(h + 1) * dh, :] += _mm(v[h * dh:(h + 1) * dh, :], ps)

    def body(kb, c):
        block(kb, False)
        return c

    lax.fori_loop(0, qi, body, 0)
    block(qi, True)
    inv_l = 1.0 / l_s[...]
    o_t = jnp.concatenate([acc_s[h * dh:(h + 1) * dh, :] * inv_l[:, h * tq:(h + 1) * tq] for h in range(hg)], axis=0)
    o_ref[...] = o_t.T.astype(o_ref.dtype)


def _fox_prompt_call(qt, k, vt, ct, *, tq):
    B, C, S = qt.shape
    T = B * S
    n_pieces = 2 if qt.dtype == F32 else 1
    H = ct.shape[1]
    dh = C // H
    hg = MXU_DIM // dh
    ng = H // hg
    nq = S // tq
    width = hg * dh
    assert tq % LANES == 0
    ccol = jnp.transpose(ct.reshape(B, ng, hg, S), (0, 1, 3, 2))
    kern = functools.partial(_fox_prompt_kernel, tq=tq, hg=hg, dh=dh)
    return pl.pallas_call(
        kern, grid=(B, ng, nq),
        in_specs=[pl.BlockSpec((1, width, tq), lambda b, g, i: (b, g, i)),
                  pl.BlockSpec((S, width), lambda b, g, i: (b, g)),
                  pl.BlockSpec((1, width, S), lambda b, g, i: (b, g, 0)),
                  pl.BlockSpec((1, 1, S, hg), lambda b, g, i: (b, g, 0, 0))],
        out_specs=pl.BlockSpec((tq, width), lambda b, g, i: (b * nq + i, g)),
        out_shape=jax.ShapeDtypeStruct((T, C), qt.dtype),
        scratch_shapes=[pltpu.VMEM((n_pieces, width, hg * tq), BF16), pltpu.VMEM((tq, hg * tq), F32),
                        pltpu.VMEM((n_pieces, tq, hg * tq), BF16), pltpu.VMEM((1, hg * tq), F32),
                        pltpu.VMEM((1, hg * tq), F32), pltpu.VMEM((width, tq), F32)],
        compiler_params=_cparams(("arbitrary", "arbitrary", "arbitrary")), name="fox_prompt",
    )(qt, k, vt, ccol)


def _fox_decode_kernel(pt_ref, q_ref, kn_ref, vn_ref, cn_ref, *rest, pp, H, dh, L, page):
    kt_refs = rest[:pp]
    vt_refs = rest[pp:2 * pp]
    lf_refs = rest[2 * pp:3 * pp]
    o_ref = rest[3 * pp]
    kb_s, vb_s, knp_s, vnp_s, m_s, l_s, acc_s, dsum_s = rest[3 * pp + 1:]
    j = pl.program_id(1)
    nj = pl.num_programs(1)
    C = H * dh
    R = L * H

    q = q_ref[0].astype(F32)
    head_mask = (_iota((1, C), 1) // dh) == _iota((H, 1), 0)
    qs = jnp.concatenate([jnp.where(head_mask, q[t:t + 1, :], 0.0) for t in range(L)], axis=0).astype(BF16)

    @pl.when(j == 0)
    def _():
        knp_s[...] = jnp.zeros_like(knp_s)
        vnp_s[...] = jnp.zeros_like(vnp_s)
        knp_s[0:L, :] = kn_ref[0]
        vnp_s[0:L, :] = vn_ref[0]
        s = _dot_nt(qs, knp_s[...].astype(BF16)) - cn_ref[0]
        s = jnp.where(_iota((R, LANES), 1) <= _iota((R, LANES), 0) // H, s, NEG_INF)
        m_new = jnp.max(s, axis=-1, keepdims=True)
        p = jnp.exp(s - m_new)
        m_s[...] = m_new
        l_s[...] = jnp.sum(p, axis=-1, keepdims=True)
        acc_s[...] = _dot(p.astype(BF16), vnp_s[...].astype(BF16))
        dsum_s[...] = jnp.zeros_like(dsum_s)

    for i in range(pp):
        kb_s[:, i * page:(i + 1) * page] = kt_refs[i][0].astype(BF16)
        vb_s[:, i * page:(i + 1) * page] = vt_refs[i][0].astype(BF16)
    lf_all = jnp.concatenate([lf_refs[i][0] for i in range(pp)], axis=0)
    later = (_iota((page, page), 0) > _iota((page, page), 1)).astype(BF16)
    suffix = _dot_exact_rhs01(lf_all, later)
    total = jnp.sum(lf_all, axis=-1, keepdims=True)
    dsum = dsum_s[...]
    parts = []
    for i in range(pp):
        d_i = suffix[i * H:(i + 1) * H] + dsum
        parts.append(jnp.concatenate([d_i] * L, axis=0))
        dsum = dsum + total[i * H:(i + 1) * H]
    dsum_s[...] = dsum
    d = jnp.concatenate(parts, axis=1) if pp > 1 else parts[0]
    s = _dot(qs, kb_s[...]) + d
    m_old = m_s[...]
    m_new = jnp.maximum(m_old, jnp.max(s, axis=-1, keepdims=True))
    alpha = jnp.exp(m_old - m_new)
    p = jnp.exp(s - m_new)
    l_s[...] = alpha * l_s[...] + jnp.sum(p, axis=-1, keepdims=True)
    m_s[...] = m_new
    acc_s[...] = alpha * acc_s[...] + _dot_nt(p.astype(BF16), vb_s[...])

    @pl.when(j == nj - 1)
    def _():
        o = acc_s[...] * (1.0 / l_s[...])
        keep = (_iota((1, C), 1) // dh) == (_iota((R, 1), 0) % H)
        om = jnp.where(keep, o, 0.0).astype(BF16)
        pick = (_iota((SUBLANES_BF16, R), 1) // H == _iota((SUBLANES_BF16, R), 0)).astype(BF16)
        o_ref[0] = _dot(pick, om)[0:L]


def _fox_decode_call(page_table, q, k_new, v_new, c_new, kt_pages, vt_pages, lf_pages, *, page_base, pp):
    Bs, L, C = q.shape
    H = c_new.shape[-1]
    dh = C // H
    n_pages = page_table.shape[1]
    page = kt_pages.shape[2]
    assert n_pages % pp == 0 and L * H <= LANES and L <= SUBLANES_BF16
    nj = n_pages // pp
    R = L * H
    cn = jnp.broadcast_to(jnp.transpose(c_new, (0, 2, 1))[:, None], (Bs, L, H, L)).reshape(Bs, R, L)
    cn = jnp.pad(cn, ((0, 0), (0, 0), (0, LANES - L)))
    seq = lambda w: pl.BlockSpec((1, L, w), lambda b, j, pt: (b, 0, 0))

    def page_spec(i, rows):
        return pl.BlockSpec((1, rows, page),
                            lambda b, j, pt: (page_base + pt[b, n_pages - 1 - (j * pp + i)], 0, 0))

    in_specs = ([seq(C), seq(C), seq(C), pl.BlockSpec((1, R, LANES), lambda b, j, pt: (b, 0, 0))]
                + [page_spec(i, C) for i in range(pp)] + [page_spec(i, C) for i in range(pp)]
                + [page_spec(i, H) for i in range(pp)])
    kern = functools.partial(_fox_decode_kernel, pp=pp, H=H, dh=dh, L=L, page=page)
    grid_spec = pltpu.PrefetchScalarGridSpec(
        num_scalar_prefetch=1, grid=(Bs, nj), in_specs=in_specs,
        out_specs=pl.BlockSpec((1, L, C), lambda b, j, pt: (b, 0, 0)),
        scratch_shapes=[pltpu.VMEM((C, pp * page), BF16), pltpu.VMEM((C, pp * page), BF16),
                        pltpu.VMEM((LANES, C), F32), pltpu.VMEM((LANES, C), F32),
                        pltpu.VMEM((R, 1), F32), pltpu.VMEM((R, 1), F32),
                        pltpu.VMEM((R, C), F32), pltpu.VMEM((H, 1), F32)])
    return pl.pallas_call(
        kern, grid_spec=grid_spec, out_shape=jax.ShapeDtypeStruct((Bs, L, C), F32),
        compiler_params=_cparams(("arbitrary", "arbitrary")), name="fox_decode",
    )(page_table, q, k_new, v_new, cn, *([kt_pages] * pp), *([vt_pages] * pp), *([lf_pages] * pp))


def _mem_kv_kernel(m_ref, g_ref, w_ref, kg_ref, g128_ref, k_ref, v_ref, kb_ref, vb_ref, *, C):
    hp = w_ref.shape[0] > 1
    act = (lambda v: v) if hp else (lambda v: v.astype(BF16))
    u = _parts(act(_rms(m_ref[...], g_ref[...])))
    zk = _mm(u, _wparts(w_ref, slice(None), slice(0, C)))
    ms = _mm(act(zk * zk), g128_ref[...])
    kn = zk * lax.rsqrt(ms + EPS) * kg_ref[...]
    k_ref[...] = kn
    kb_ref[...] = kn.astype(BF16)
    zv = _mm(u, _wparts(w_ref, slice(None), slice(C, 2 * C)))
    v_ref[...] = zv
    vb_ref[...] = zv.astype(BF16)


def _mem_kv_call(mem, g, w, k_gain, g128, *, tm):
    T, D = mem.shape
    C = w.shape[2] // 2
    tok = lambda wd: pl.BlockSpec((tm, wd), lambda i: (i, 0))
    return pl.pallas_call(
        functools.partial(_mem_kv_kernel, C=C), grid=(T // tm,),
        in_specs=[tok(D), _const_spec((1, D)), _const_spec(w.shape), _const_spec((1, C)), _const_spec((C, C))],
        out_specs=[tok(C)] * 4,
        out_shape=[jax.ShapeDtypeStruct((T, C), F32), jax.ShapeDtypeStruct((T, C), F32),
                   jax.ShapeDtypeStruct((T, C), BF16), jax.ShapeDtypeStruct((T, C), BF16)],
        compiler_params=_cparams(("arbitrary",)), name="mem_kv",
    )(mem, g, w, k_gain, g128)


def _mem_attn_kernel(q_ref, k_ref, v_ref, o_ref, *, heads, dh):
    q = q_ref[0]
    hp = q.dtype == F32
    outs = []
    for h in range(heads):
        sl = slice(h * dh, (h + 1) * dh)
        s = _mm(q[:, sl], k_ref[0, :, sl], "nt")
        p = jnp.exp(s - jnp.max(s, axis=-1, keepdims=True))
        l = jnp.sum(p, axis=-1, keepdims=True)
        outs.append(_mm(p if hp else p.astype(BF16), v_ref[0, :, sl]) * (1.0 / l))
    o_ref[0] = jnp.concatenate(outs, axis=1).astype(o_ref.dtype)


def _mem_attn_call(q, k, v, *, heads, tq):
    B, S, C = q.shape
    M = k.shape[1]
    tq = min(tq, S)
    return pl.pallas_call(
        functools.partial(_mem_attn_kernel, heads=heads, dh=C // heads), grid=(B, S // tq),
        in_specs=[pl.BlockSpec((1, tq, C), lambda b, i: (b, i, 0)),
                  pl.BlockSpec((1, M, C), lambda b, i: (b, 0, 0)),
                  pl.BlockSpec((1, M, C), lambda b, i: (b, 0, 0))],
        out_specs=pl.BlockSpec((1, tq, C), lambda b, i: (b, i, 0)),
        out_shape=jax.ShapeDtypeStruct((B, S, C), q.dtype),
        compiler_params=_cparams(("arbitrary", "arbitrary")), name="mem_attn",
    )(q, k, v)


def _sorted_rows(tm, n_exp):
    worst = 2 * tm + n_exp * (MOE_BLOCK - 1)
    return -(-worst // MXU_DIM) * MXU_DIM


def _pick_matrix(pos_row, rows):
    return jnp.where(_iota((rows, pos_row.shape[1]), 0) == pos_row.astype(jnp.int32), 1.0, 0.0).astype(BF16)


def _merge_kernel(x_ref, op_ref, ol_ref, oa_ref, om_ref, g_ref, wg_ref, bg_ref, wb_ref, wo_ref,
                  gf_ref, wr_ref, br_ref, x1_ref, xs_ref, pos_ref, run_ref,
                  *, D, n_groups, per_group, tm, rows):
    x = x_ref[...]
    hp = wb_ref.shape[0] > 1
    act = (lambda v: v) if hp else (lambda v: v.astype(BF16))
    u = _parts(act(_rms(x, g_ref[...])))
    merged = None
    for n, o_ref in enumerate((op_ref, ol_ref, oa_ref, om_ref)):
        cols = slice(n * D, (n + 1) * D)
        gate = jax.nn.sigmoid(_mm(u, _wparts(wg_ref, slice(None), cols)) + bg_ref[:, cols])
        term = gate * _mm(o_ref[...], _wparts(wb_ref, n))
        merged = term if merged is None else merged + term
    x1 = x + _mm(act(merged), _wparts(wo_ref))
    x1_ref[...] = x1
    t = _rms(x1, gf_ref[...])
    tb = t.astype(BF16)
    t_lo = (t - tb.astype(F32)).astype(BF16)
    w_hi = wr_ref[0]
    w_lo = wr_ref[1]
    logit = _dot_nt(w_hi, tb) + _dot_nt(w_hi, t_lo) + _dot_nt(w_lo, tb) + br_ref[...]
    n_exp = n_groups * per_group
    rid = _iota(logit.shape, 0)
    big = jnp.int32(1 << 20)
    is_g = (rid >= n_exp) & (rid < n_exp + n_groups)
    gl = jnp.where(is_g, logit, NEG_INF)
    gmax = jnp.max(gl, axis=0, keepdims=True)
    gidx = jnp.min(jnp.where(gl == gmax, rid, big), axis=0, keepdims=True) - n_exp
    g_w = 1.0 / jnp.sum(jnp.where(is_g, jnp.exp(gl - gmax), 0.0), axis=0, keepdims=True)
    in_grp = (rid >= gidx * per_group) & (rid < (gidx + 1) * per_group)
    el = jnp.where(in_grp, logit, NEG_INF)
    v1 = jnp.max(el, axis=0, keepdims=True)
    i1 = jnp.min(jnp.where(el == v1, rid, big), axis=0, keepdims=True)
    el2 = jnp.where(rid == i1, NEG_INF, el)
    v2 = jnp.max(el2, axis=0, keepdims=True)
    i2 = jnp.min(jnp.where(el2 == v2, rid, big), axis=0, keepdims=True)
    e21 = jnp.exp(v2 - v1)
    w1 = g_w / (1.0 + e21)
    w2 = g_w * e21 / (1.0 + e21)

    sel1 = rid == i1
    sel2 = rid == i2
    member = jnp.where(sel1 | sel2, 1.0, 0.0)
    count = jnp.sum(member, axis=1, keepdims=True)
    padded = jnp.floor((count + (MOE_BLOCK - 1)) * (1.0 / MOE_BLOCK)) * MOE_BLOCK
    below = jnp.where(_iota((LANES, LANES), 1) < _iota((LANES, LANES), 0), 1.0, 0.0).astype(BF16)
    start = _dot(below, jnp.broadcast_to(padded, (LANES, LANES)).astype(BF16))[:, 0:1]
    earlier = jnp.where(_iota((tm, tm), 0) < _iota((tm, tm), 1), 1.0, 0.0).astype(BF16)
    rank = _dot(member.astype(BF16), earlier)
    where_to = start + rank
    pos1 = jnp.sum(jnp.where(sel1, where_to, 0.0), axis=0, keepdims=True)
    pos2 = jnp.sum(jnp.where(sel2, where_to, 0.0), axis=0, keepdims=True)
    pick1 = _pick_matrix(pos1, rows)
    pick2 = _pick_matrix(pos2, rows)
    xs_ref[:, 0:D] = _dot(pick1 + pick2, tb).astype(BF16)
    r8 = _iota((LANES, tm), 0)

    def pieces(w):
        hi, mid, lo = _split3(w)
        return jnp.where(r8 == 0, hi.astype(F32), jnp.where(r8 == 1, mid.astype(F32),
                         jnp.where(r8 == 2, lo.astype(F32), 0.0))).astype(BF16)

    xs_ref[:, D:D + LANES] = (_dot_nt(pick1, pieces(w1)) + _dot_nt(pick2, pieces(w2))).astype(BF16)
    r8p = _iota((8, tm), 0)
    pos_ref[0] = jnp.where(r8p == 0, pos1, jnp.where(r8p == 1, pos2, 0.0))
    eye = _iota((LANES, LANES), 0) == _iota((LANES, LANES), 1)
    to_row = lambda col: jnp.sum(jnp.where(eye, col, 0.0), axis=0, keepdims=True)
    r8r = _iota((8, LANES), 0)
    run_ref[0] = jnp.where(r8r == 0, to_row(padded), jnp.where(r8r == 1, to_row(start), 0.0))


def _merge_call(x, o_pool, o_lru, o_att, o_mem, lw, *, tm, n_groups, per_group):
    T, D = x.shape
    C = D // 2
    n_tiles = T // tm
    rows = _sorted_rows(tm, n_groups * per_group)
    tok = lambda w: pl.BlockSpec((tm, w), lambda i: (i, 0))
    kern = functools.partial(_merge_kernel, D=D, n_groups=n_groups, per_group=per_group, tm=tm, rows=rows)
    return pl.pallas_call(
        kern, grid=(n_tiles,),
        in_specs=[tok(D), tok(C), tok(C), tok(C), tok(C), _const_spec((1, D)), _const_spec(lw['w_g'].shape),
                  _const_spec((1, 4 * D)), _const_spec(lw['w_branch'].shape), _const_spec(lw['w_out'].shape),
                  _const_spec((1, D)),
                  _const_spec((2, LANES, D)), _const_spec((LANES, 1))],
        out_specs=[tok(D), pl.BlockSpec((rows, D + LANES), lambda i: (i, 0)),
                   pl.BlockSpec((1, 8, tm), lambda i: (i, 0, 0)), pl.BlockSpec((1, 8, LANES), lambda i: (i, 0, 0))],
        out_shape=[jax.ShapeDtypeStruct((T, D), F32), jax.ShapeDtypeStruct((n_tiles * rows, D + LANES), BF16),
                   jax.ShapeDtypeStruct((n_tiles, 8, tm), F32), jax.ShapeDtypeStruct((n_tiles, 8, LANES), F32)],
        compiler_params=_cparams(("arbitrary",)), name="merge",
    )(x, o_pool, o_lru, o_att, o_mem, lw['norm_mix'], lw['w_g'], lw['b_g'], lw['w_branch'], lw['w_out'],
      lw['norm_ffn'], lw['w_route'], lw['b_route'])


def _block_tables(run_info, *, n_exp, rows):
    n_tiles = run_info.shape[0]
    nblk = (run_info[:, 0, :n_exp] / MOE_BLOCK).astype(jnp.int32).T
    first = (run_info[:, 1, :n_exp] / MOE_BLOCK).astype(jnp.int32).T + \
        (jnp.arange(n_tiles, dtype=jnp.int32) * (rows // MOE_BLOCK))[None, :]
    nb = MOE_CHUNK_BLOCKS
    run_end = jnp.cumsum(nblk, axis=1)
    per_expert = run_end[:, -1]
    chunks_e = (per_expert + nb - 1) // nb
    chunk_end = jnp.cumsum(chunks_e)
    n_chunks = chunk_end[-1]
    max_chunks = n_tiles * (rows // MOE_BLOCK) // nb + n_exp
    cid = jnp.arange(max_chunks, dtype=jnp.int32)
    experts = jnp.arange(n_exp, dtype=jnp.int32)
    chunk_expert = jnp.sum((chunk_end[None, :] <= cid[:, None]).astype(jnp.int32), axis=1)
    last_real = jnp.max(jnp.where(chunks_e > 0, experts, 0))
    chunk_expert = jnp.where(cid < n_chunks, chunk_expert, last_real)
    mine = chunk_expert[:, None] == experts[None, :]

    def of_chunk(per_e):
        hot = mine.reshape(mine.shape + (1,) * (per_e.ndim - 1))
        return jnp.sum(jnp.where(hot, per_e[None], 0), axis=1)

    q = ((cid - of_chunk(chunk_end - chunks_e)) * nb)[:, None] + jnp.arange(nb, dtype=jnp.int32)[None, :]
    valid = (cid < n_chunks)[:, None] & (q < of_chunk(per_expert)[:, None])
    ends = of_chunk(run_end)[:, None, :]
    done = ends <= q[:, :, None]
    tile = jnp.minimum(jnp.sum(done.astype(jnp.int32), axis=2), n_tiles - 1)
    before = jnp.max(jnp.where(done, ends, 0), axis=2)
    tiles = jnp.arange(n_tiles, dtype=jnp.int32)[None, None, :]
    run_first = jnp.sum(jnp.where(tiles == tile[:, :, None], of_chunk(first)[:, None, :], 0), axis=2)
    src = jnp.where(valid, run_first + (q - before), -1).astype(jnp.int32).reshape(-1)
    return chunk_expert.astype(jnp.int32), src, n_chunks.reshape(1).astype(jnp.int32)


def _expert_kernel(ce_ref, src_ref, nc_ref, xs_in, wgu_ref, wd_ref, xs_out, xbuf, ybuf, in_sem, out_sem,
                   *, D, F):
    del xs_in
    c = pl.program_id(0)
    n = nc_ref[0]
    nb = MOE_CHUNK_BLOCKS
    blk = MOE_BLOCK

    def in_copy(chunk, slot, k):
        row = pl.multiple_of(src_ref[chunk * nb + k] * blk, blk)
        return pltpu.make_async_copy(xs_out.at[pl.ds(row, blk), :], xbuf.at[slot, pl.ds(k * blk, blk), :],
                                     in_sem.at[slot])

    def out_copy(chunk, slot, k):
        row = pl.multiple_of(src_ref[chunk * nb + k] * blk, blk)
        return pltpu.make_async_copy(ybuf.at[slot, pl.ds(k * blk, blk), :],
                                     xs_out.at[pl.ds(row, blk), pl.ds(0, D)], out_sem.at[slot])

    def for_valid(chunk, fn):
        for k in range(nb):
            @pl.when(src_ref[chunk * nb + k] >= 0)
            def _():
                fn(k)

    @pl.when(c == 0)
    def _():
        xbuf[...] = jnp.zeros_like(xbuf)
        for_valid(0, lambda k: in_copy(0, 0, k).start())

    @pl.when(c < n)
    def _():
        slot = c % 2
        for_valid(c, lambda k: in_copy(c, slot, k).wait())

        @pl.when(c + 1 < n)
        def _():
            for_valid(c + 1, lambda k: in_copy(c + 1, 1 - slot, k).start())

        @pl.when(c >= 2)
        def _():
            for_valid(c - 2, lambda k: out_copy(c - 2, slot, k).wait())

        xin = xbuf[slot]
        x = xin[:, 0:D]
        wp = xin[:, D:D + LANES].astype(F32)
        w = wp[:, 0:1] + wp[:, 1:2] + wp[:, 2:3]
        gu = _dot(x, wgu_ref[0])
        h = _silu(gu[:, :F]) * gu[:, F:]
        ybuf[slot] = _dot((h * w).astype(BF16), wd_ref[0]).astype(BF16)
        for_valid(c, lambda k: out_copy(c, slot, k).start())

        @pl.when(c == n - 1)
        def _():
            for_valid(c, lambda k: out_copy(c, slot, k).wait())

            @pl.when(c >= 1)
            def _():
                for_valid(c - 1, lambda k: out_copy(c - 1, 1 - slot, k).wait())


def _expert_call(xs, chunk_expert, src, n_chunks, wgu, wd):
    n_rows, width = xs.shape
    E, D, F2 = wgu.shape
    F = F2 // 2
    max_chunks = chunk_expert.shape[0]
    chunk_rows = MOE_CHUNK_BLOCKS * MOE_BLOCK
    grid_spec = pltpu.PrefetchScalarGridSpec(
        num_scalar_prefetch=3, grid=(max_chunks,),
        in_specs=[pl.BlockSpec(memory_space=pl.ANY),
                  pl.BlockSpec((1, D, F2), lambda c, ce, src, nc: (ce[c], 0, 0)),
                  pl.BlockSpec((1, F, D), lambda c, ce, src, nc: (ce[c], 0, 0))],
        out_specs=pl.BlockSpec(memory_space=pl.ANY),
        scratch_shapes=[pltpu.VMEM((2, chunk_rows, width), BF16), pltpu.VMEM((2, chunk_rows, D), BF16),
                        pltpu.SemaphoreType.DMA((2,)), pltpu.SemaphoreType.DMA((2,))])
    return pl.pallas_call(
        functools.partial(_expert_kernel, D=D, F=F), grid_spec=grid_spec,
        out_shape=jax.ShapeDtypeStruct((n_rows, width), BF16),
        input_output_aliases={3: 0},
        compiler_params=_cparams(("arbitrary",)), name="experts",
    )(chunk_expert, src, n_chunks, xs, wgu, wd)


def _combine_kernel(x1_ref, ys_ref, pos_ref, y_ref, *, D, rows):
    pos = pos_ref[0]
    pick = _pick_matrix(pos[0:1, :], rows) + _pick_matrix(pos[1:2, :], rows)
    y_ref[...] = x1_ref[...] + _dot_tn(pick, ys_ref[:, 0:D])


def _combine_call(x1, ys, pos, *, tm, rows):
    T, D = x1.shape
    width = ys.shape[1]
    return pl.pallas_call(
        functools.partial(_combine_kernel, D=D, rows=rows), grid=(T // tm,),
        in_specs=[pl.BlockSpec((tm, D), lambda i: (i, 0)), pl.BlockSpec((rows, width), lambda i: (i, 0)),
                  pl.BlockSpec((1, 8, tm), lambda i: (i, 0, 0))],
        out_specs=pl.BlockSpec((tm, D), lambda i: (i, 0)),
        out_shape=jax.ShapeDtypeStruct((T, D), F32),
        compiler_params=_cparams(("arbitrary",)), name="combine",
    )(x1, ys, pos)


PIECED_WEIGHTS = ('w_a', 'w_t', 'pool_w', 'wa', 'wx', 'w_mem_kv', 'w_g', 'w_branch', 'w_out')


def _single_pass(lw):
    return {k: (v[:1] if k in PIECED_WEIGHTS else v) for k, v in lw.items()}


def _prep_layer(l, p, *, C, H, mem_heads, high_precision):
    D = 2 * C
    pieces = functools.partial(_stack_parts, high_precision=high_precision)
    w_in, b_in = p['w_in'][l], p['b_in'][l]
    f0 = 6 * C
    w_a = pieces(jnp.concatenate([w_in[:, :f0], w_in[:, f0 + H:f0 + H + C], w_in[:, f0:f0 + H],
                                  jnp.zeros((D, LANES - H), F32)], axis=1))
    b_a = jnp.concatenate([b_in[:f0], b_in[f0 + H:f0 + H + C], b_in[f0:f0 + H], jnp.zeros((LANES - H,), F32)])[None]
    w_t = pieces(jnp.concatenate([w_in[:, 3 * C:6 * C].T, w_in[:, f0:f0 + H].T,
                                  jnp.zeros((SUBLANES_BF16 - H, D), F32)], axis=0))
    b_t = jnp.concatenate([b_in[3 * C:6 * C], b_in[f0:f0 + H], jnp.zeros((SUBLANES_BF16 - H,), F32)])[:, None]
    g0 = f0 + H + C
    n_exp = p['w_router'].shape[2]
    n_grp = p['w_group'].shape[2]
    w_r = jnp.concatenate([p['w_router'][l].T, p['w_group'][l].T,
                           jnp.zeros((LANES - n_exp - n_grp, D), F32)], axis=0)
    w_r_hi, w_r_lo = _stack_parts(w_r, True)
    b_r = jnp.concatenate([p['b_router'][l], p['b_group'][l], jnp.zeros((LANES - n_exp - n_grp,), F32)])[:, None]
    wg = p['w_gate'][l].reshape((n_exp,) + p['w_gate'].shape[3:])
    wu = p['w_up'][l].reshape((n_exp,) + p['w_up'].shape[3:])
    wd = p['w_down'][l].reshape((n_exp,) + p['w_down'].shape[3:])
    dh = C // H
    mdh = C // mem_heads
    k_gain = jnp.tile(p['k_norm'][l], H)
    q_gain = jnp.tile(p['q_norm'][l], H)
    return dict(
        mem_head_dim=mdh,
        norm_mix=p['norm_mix'][l][None], w_a=w_a, b_a=b_a, w_t=w_t, b_t=b_t,
        pool_w=pieces(_block_diag(p['pool_w'][l], MXU_DIM)), pool_scale=p['pool_scale'][l][None],
        conv_w=p['conv_w'][l], conv_b=p['conv_b'][l][None],
        wa=pieces(_block_diag(p['lru_wa'][l], MXU_DIM)), ba=p['lru_ba'][l][None],
        wx=pieces(_block_diag(p['lru_wx'][l], MXU_DIM)), bx=p['lru_bx'][l][None],
        lam=p['lru_lambda'][l][None],
        q_gain=q_gain[None], q_gain_col=q_gain[:, None], k_gain=k_gain[None], k_gain_col=k_gain[:, None],
        mq_gain=jnp.tile(p['mq_norm'][l], mem_heads)[None], mk_gain=jnp.tile(p['mk_norm'][l], mem_heads)[None],
        g64=_group_mean_matrix(C, dh), g128=_group_mean_matrix(C, mdh),
        mem_norm=p['mem_norm'][l][None], w_mem_kv=pieces(p['w_mem_kv'][l]),
        w_g=pieces(w_in[:, g0:]), b_g=b_in[g0:][None],
        w_branch=pieces(p['w_branch'][l]), w_out=pieces(p['w_out'][l]),
        norm_ffn=p['norm_ffn'][l][None], w_route=jnp.stack([w_r_hi, w_r_lo]), b_route=b_r,
        w_gu=jnp.concatenate([wg, wu], axis=2).astype(BF16), w_d=wd.astype(BF16),
        n_groups=n_grp, per_group=n_exp // n_grp,
    )


def _tail(x, o_pool, o_lru, o_att, o_mem, lw, *, tm):
    n_exp = lw['n_groups'] * lw['per_group']
    rows = _sorted_rows(tm, n_exp)
    x1, xs, pos, run_info = _merge_call(x, o_pool, o_lru, o_att, o_mem, lw, tm=tm,
                                        n_groups=lw['n_groups'], per_group=lw['per_group'])
    chunk_expert, src, n_chunks = _block_tables(run_info, n_exp=n_exp, rows=rows)
    ys = _expert_call(xs, chunk_expert, src, n_chunks, lw['w_gu'], lw['w_d'])
    return _combine_call(x1, ys, pos, tm=tm, rows=rows)


def kernel(x_prompt, x_sample, cache_k, cache_v, cache_logf, cache_mem_k, cache_mem_v, state_pool, state_conv,
           state_lru, page_table, mem_prompt, norm_mix, w_in, b_in, pool_w, pool_scale, conv_w, conv_b, lru_wa,
           lru_ba, lru_wx, lru_bx, lru_lambda, q_norm, k_norm, mq_norm, mem_norm, w_mem_kv, mk_norm, w_branch,
           w_out, norm_ffn, w_group, b_group, w_router, b_router, w_gate, w_up, w_down):
    p = dict(norm_mix=norm_mix, w_in=w_in, b_in=b_in, pool_w=pool_w, pool_scale=pool_scale, conv_w=conv_w,
             conv_b=conv_b, lru_wa=lru_wa, lru_ba=lru_ba, lru_wx=lru_wx, lru_bx=lru_bx, lru_lambda=lru_lambda,
             q_norm=q_norm, k_norm=k_norm, mq_norm=mq_norm, mem_norm=mem_norm, w_mem_kv=w_mem_kv, mk_norm=mk_norm,
             w_branch=w_branch, w_out=w_out, norm_ffn=norm_ffn, w_group=w_group, b_group=b_group,
             w_router=w_router, b_router=b_router, w_gate=w_gate, w_up=w_up, w_down=w_down)
    depth = w_in.shape[0]
    B, S, D = x_prompt.shape
    Bs, L, _ = x_sample.shape
    C = D // 2
    _, n_pool, page, H, dh = cache_k.shape
    M, mem_heads, mdh = cache_mem_k.shape[2:]
    n_pages = page_table.shape[1]
    past_len = n_pages * page
    n_pool_rows = state_pool.shape[2]
    n_conv_rows = state_conv.shape[2]
    Tp, Ts = B * S, Bs * L
    tm = TOKEN_TILE
    tm_s = min(tm, Ts)

    xp = x_prompt.reshape(Tp, D)
    xs = x_sample.reshape(Ts, D)
    memp = mem_prompt.reshape(B * M, D)
    pool0 = jnp.zeros((B, POOL_HIST, C), F32)
    conv0 = jnp.zeros((B, CONV_HIST, C), F32)
    h0 = jnp.zeros((B, 1, C), F32)
    kt_pages = jnp.transpose(cache_k, (0, 1, 3, 4, 2)).reshape(depth * n_pool, C, page)
    vt_pages = jnp.transpose(cache_v, (0, 1, 3, 4, 2)).reshape(depth * n_pool, C, page)
    lf_pages = jnp.transpose(cache_logf, (0, 1, 3, 2)).reshape(depth * n_pool, H, page)

    outs = {n: [] for n in ('kp', 'vp', 'lfp', 'mkp', 'mvp', 'poolp', 'convp', 'lrup',
                            'ks', 'vs', 'lfs', 'pools', 'convs', 'lrus')}
    for l in range(depth):
        hp = l < depth - 1
        lw = _prep_layer(l, p, C=C, H=H, mem_heads=mem_heads, high_precision=hp)
        lw_s = _single_pass(lw)
        mk, mv, mkb, mvb = _mem_kv_call(memp, lw['mem_norm'], lw['w_mem_kv'], lw['mk_gain'], lw['g128'], tm=512)
        (o_pool, o_lru, mq, qt, kt, vt, lft, ct, k_rows, pool_n, conv_n, h_n) = _mix_call(
            xp, lw, pool0, conv0, h0, seq_len=S, pos0=0, H=H, tm=tm, transposed_kv=True)
        o_att = _fox_prompt_call(qt, k_rows, vt, ct, tq=256)
        o_mem = _mem_attn_call(mq.reshape(B, S, C), (mk if hp else mkb).reshape(B, M, C),
                               (mv if hp else mvb).reshape(B, M, C), heads=mem_heads, tq=512).reshape(Tp, C)
        xp = _tail(xp, o_pool, o_lru, o_att, o_mem, lw, tm=tm)
        outs['kp'].append(jnp.transpose(kt.reshape(B, H, dh, S), (0, 3, 1, 2)))
        outs['vp'].append(jnp.transpose(vt.reshape(B, H, dh, S), (0, 3, 1, 2)))
        outs['lfp'].append(jnp.transpose(lft, (0, 2, 1)))
        outs['mkp'].append(mk.reshape(B, M, mem_heads, mdh)); outs['mvp'].append(mv.reshape(B, M, mem_heads, mdh))
        outs['poolp'].append(pool_n[:, POOL_HIST - n_pool_rows:]); outs['convp'].append(conv_n[:, CONV_HIST - n_conv_rows:])
        outs['lrup'].append(h_n[:, 0])
        pool_prev = jnp.pad(state_pool[l], ((0, 0), (POOL_HIST - n_pool_rows, 0), (0, 0)))
        conv_prev = jnp.pad(state_conv[l], ((0, 0), (CONV_HIST - n_conv_rows, 0), (0, 0)))
        (o_pool, o_lru, mq, q, k, v, lf, c, pool_n, conv_n, h_n) = _mix_call(
            xs, lw_s, pool_prev, conv_prev, state_lru[l][:, None], seq_len=L, pos0=past_len, H=H, tm=tm_s,
            transposed_kv=False)
        o_att = _fox_decode_call(page_table, q.reshape(Bs, L, C), k.reshape(Bs, L, C), v.reshape(Bs, L, C),
                                 c.reshape(Bs, L, H), kt_pages, vt_pages, lf_pages,
                                 page_base=l * n_pool, pp=8).reshape(Ts, C).astype(BF16)
        o_mem = _mem_attn_call(mq.reshape(Bs, L, C), cache_mem_k[l].reshape(Bs, M, C).astype(BF16),
                               cache_mem_v[l].reshape(Bs, M, C).astype(BF16), heads=mem_heads, tq=L).reshape(Ts, C)
        xs = _tail(xs, o_pool, o_lru, o_att, o_mem, lw_s, tm=tm_s)
        outs['ks'].append(k.reshape(Bs, L, H, dh)); outs['vs'].append(v.reshape(Bs, L, H, dh))
        outs['lfs'].append(lf.reshape(Bs, L, H))
        outs['pools'].append(pool_n[:, POOL_HIST - n_pool_rows:]); outs['convs'].append(conv_n[:, CONV_HIST - n_conv_rows:])
        outs['lrus'].append(h_n[:, 0])

    st = lambda n: jnp.stack(outs[n])
    return (xp.reshape(B, S, D), xs.reshape(Bs, L, D),
            st('kp'), st('vp'), st('lfp'), st('mkp'), st('mvp'), st('poolp'), st('convp'), st('lrup'),
            st('ks'), st('vs'), st('lfs'), st('pools'), st('convs'), st('lrus'))
```

```python
import functools
import math

import jax
import jax.numpy as jnp
import numpy as np
from jax import lax
from jax.experimental import pallas as pl
from jax.experimental.pallas import tpu as pltpu

F32 = jnp.float32
BF16 = jnp.bfloat16

EPS = 1e-6
LRU_C = 8.0
POOL_WINDOWS = (2, 4, 8, 16)
POOL_HIST = 16
CONV_WIDTH = 4
CONV_HIST = 8
LANES = 128
SUBLANES_BF16 = 16
MXU_DIM = 256
VMEM_LIMIT = 56 * 1024 * 1024
TOKEN_TILE = 256
MOE_BLOCK = SUBLANES_BF16
MOE_CHUNK_BLOCKS = 32
DECODE_PAGES_PER_STEP = 16

NEG_INF = float("-inf")


def _dot(a, b):
    return jnp.dot(a, b, preferred_element_type=F32)


def _dot_nt(a, b):
    return lax.dot_general(a, b, (((1,), (1,)), ((), ())), preferred_element_type=F32)


def _dot_tn(a, b):
    return lax.dot_general(a, b, (((0,), (0,)), ((), ())), preferred_element_type=F32)


def _split3(x):
    hi = x.astype(BF16)
    r1 = x - hi.astype(F32)
    mid = r1.astype(BF16)
    lo = (r1 - mid.astype(F32)).astype(BF16)
    return hi, mid, lo


def _dot_exact_rhs01(x, m01):
    hi, mid, lo = _split3(x)
    return _dot(hi, m01) + _dot(mid, m01) + _dot(lo, m01)


def _parts(x):
    if isinstance(x, tuple):
        return x
    if x.dtype == BF16:
        return (x,)
    hi = x.astype(BF16)
    return hi, (x - hi.astype(F32)).astype(BF16)


def _mm(a, b, kind="nn"):
    pa, pb = _parts(a), _parts(b)
    f = {"nn": _dot, "nt": _dot_nt, "tn": _dot_tn}[kind]
    out = f(pa[0], pb[0])
    if len(pa) > 1:
        out = out + f(pa[1], pb[0])
    if len(pb) > 1:
        out = out + f(pa[0], pb[1])
    return out


def _wparts(ref, *idx):
    return tuple(ref[(p,) + idx] for p in range(ref.shape[0]))


def _round_to_bf16(w):
    return lax.reduce_precision(w, exponent_bits=8, mantissa_bits=7)


def _stack_parts(w, high_precision):
    hi = _round_to_bf16(w)
    if not high_precision:
        return hi.astype(BF16)[None]
    return jnp.stack([hi.astype(BF16), (w - hi).astype(BF16)])


def _softplus(y):
    return jnp.maximum(y, 0.0) + jnp.log1p(jnp.exp(-jnp.abs(y)))


def _gelu_tanh(x):
    return x * (0.5 * (1.0 + jnp.tanh(math.sqrt(2.0 / math.pi) * (x + 0.044715 * (x * x * x)))))


def _silu(x):
    return x * jax.nn.sigmoid(x)


def _rms(x, gain):
    ms = jnp.mean(x * x, axis=-1, keepdims=True)
    return x * lax.rsqrt(ms + EPS) * gain


def _iota(shape, dim):
    return lax.broadcasted_iota(jnp.int32, shape, dim)


def _cparams(sem):
    return pltpu.CompilerParams(dimension_semantics=sem, vmem_limit_bytes=VMEM_LIMIT)


def _const_spec(shape):
    nd = len(shape)
    return pl.BlockSpec(shape, lambda *_: (0,) * nd, pipeline_mode=pl.Buffered(1))


def _mix_kernel(x_ref, g_ref, w_ref, b_ref, wt_ref, bt_ref, poolw_ref, pscale_ref, convw_ref, convb_ref,
                wa_ref, ba_ref, wx_ref, bx_ref, lam_ref, qg_ref, qgc_ref, kg_ref, kgc_ref, mqg_ref,
                g64_ref, g128_ref, poolprev_ref, convprev_ref, hprev_ref,
                opool_ref, olru_ref, mq_ref, q_ref, k_ref, v_ref, lf_ref, c_ref, *rest,
                C, H, tm, lb, tiles_per_seq, pos0, att_scale, mem_scale, transposed_kv):
    if transposed_kv:
        krow_ref, *rest = rest
    (poolnew_ref, convnew_ref, hnew_ref, d_s, xc_s, a_s, b_s, pool_carry, conv_carry, h_carry, c_carry) = rest
    n_sub = tm // lb
    carry = tiles_per_seq > 1
    i = pl.program_id(0)
    gw = C // len(POOL_WINDOWS)

    hp = w_ref.shape[0] > 1
    act = (lambda v: v) if hp else (lambda v: v.astype(BF16))
    u = _parts(act(_rms(x_ref[...], g_ref[...])))

    def proj(n, width=C):
        cols = slice(n * C, n * C + width)
        return _mm(u, _wparts(w_ref, slice(None), cols)) + b_ref[:, cols]

    if carry:
        seq_off = (i % tiles_per_seq) * tm

        @pl.when(i % tiles_per_seq == 0)
        def _():
            pool_carry[...] = poolprev_ref[0]
            conv_carry[...] = convprev_ref[0]
            h_carry[...] = hprev_ref[0]
            c_carry[...] = jnp.zeros_like(c_carry)
    else:
        seq_off = 0

    row = _iota((lb, 1), 0)
    pos = pos0 + seq_off + row

    zp = proj(0)
    inv_cnt = [1.0 / jnp.minimum(w, pos + 1).astype(F32) for w in POOL_WINDOWS]
    for j in range(n_sub):
        r0 = j * lb
        prev = pool_carry[...] if carry else poolprev_ref[j]
        cur = zp[r0:r0 + lb]
        ext = jnp.concatenate([prev, cur], axis=0)
        for g, w in enumerate(POOL_WINDOWS):
            s = ext[:, g * gw:(g + 1) * gw]
            sh = 1
            while sh < w:
                s = s + pltpu.roll(s, sh, axis=0)
                sh *= 2
            d_s[r0:r0 + lb, g * gw:(g + 1) * gw] = s[POOL_HIST:] * inv_cnt[g] - cur[:, g * gw:(g + 1) * gw]
        new_state = ext[lb:lb + POOL_HIST]
        if carry:
            pool_carry[...] = new_state
            poolnew_ref[0] = new_state
        else:
            poolnew_ref[j] = new_state
    d = act(d_s[...])
    y_pool = jnp.concatenate(
        [_mm(d[:, n * MXU_DIM:(n + 1) * MXU_DIM], _wparts(poolw_ref, n)) for n in range(C // MXU_DIM)], axis=1)
    opool_ref[...] = (y_pool * pscale_ref[...]).astype(opool_ref.dtype)

    zx = proj(1)
    for j in range(n_sub):
        r0 = j * lb
        prev = conv_carry[...] if carry else convprev_ref[j]
        ext = jnp.concatenate([prev, zx[r0:r0 + lb]], axis=0)
        xc = convb_ref[...]
        for t in range(CONV_WIDTH):
            sh = CONV_WIDTH - 1 - t
            e = ext if sh == 0 else pltpu.roll(ext, sh, axis=0)
            xc = xc + e[CONV_HIST:] * convw_ref[t:t + 1, :]
        xc_s[r0:r0 + lb, :] = xc
        new_state = ext[lb:lb + CONV_HIST]
        if carry:
            conv_carry[...] = new_state
            convnew_ref[0] = new_state
        else:
            convnew_ref[j] = new_state

    xc = xc_s[...]
    xcb = act(xc)
    nb = C // MXU_DIM
    ra = jnp.concatenate([_mm(xcb[:, n * MXU_DIM:(n + 1) * MXU_DIM], _wparts(wa_ref, n)) for n in range(nb)], axis=1)
    rx = jnp.concatenate([_mm(xcb[:, n * MXU_DIM:(n + 1) * MXU_DIM], _wparts(wx_ref, n)) for n in range(nb)], axis=1)
    r = jax.nn.sigmoid(ra + ba_ref[...])
    ig = jax.nn.sigmoid(rx + bx_ref[...])
    log_a = (-LRU_C) * r * _softplus(-lam_ref[...])
    a = jnp.exp(log_a)
    th = jnp.tanh(log_a)
    one_minus_a2 = (-2.0) * th / (1.0 - th)
    mult = jnp.sqrt(one_minus_a2)
    pos_tile = pos0 + seq_off + _iota((tm, 1), 0) % lb
    mult = jnp.where(pos_tile == 0, 1.0, mult)
    a_s[...] = a
    b_s[...] = mult * ig * xc

    zg = proj(2)
    for j in range(n_sub):
        r0 = j * lb
        av = a_s[r0:r0 + lb, :]
        bv = b_s[r0:r0 + lb, :]
        sh = 1
        while sh < lb:
            m = row >= sh
            ar = pltpu.roll(av, sh, axis=0)
            br = pltpu.roll(bv, sh, axis=0)
            bv = jnp.where(m, av * br + bv, bv)
            av = jnp.where(m, av * ar, av)
            sh *= 2
        hp = h_carry[...] if carry else hprev_ref[j]
        h = av * hp + bv
        olru_ref[r0:r0 + lb, :] = (h * _gelu_tanh(zg[r0:r0 + lb])).astype(olru_ref.dtype)
        if carry:
            h_carry[...] = h[lb - 1:lb]
            hnew_ref[0] = h[lb - 1:lb]
        else:
            hnew_ref[j] = h[lb - 1:lb]

    zm = proj(6)
    msm = _mm(act(zm * zm), g128_ref[...])
    mq_ref[...] = (zm * lax.rsqrt(msm + EPS) * (mqg_ref[...] * mem_scale)).astype(mq_ref.dtype)
    zk = proj(4)
    msk = _mm(act(zk * zk), g64_ref[...])
    kn = zk * lax.rsqrt(msk + EPS) * kg_ref[...]

    if not transposed_kv:
        zq = proj(3)
        msq = _mm(act(zq * zq), g64_ref[...])
        q_ref[...] = (zq * lax.rsqrt(msq + EPS) * (qg_ref[...] * att_scale)).astype(q_ref.dtype)
        k_ref[...] = kn
        v_ref[...] = proj(5)
        zf = proj(7, LANES)
        lf = -_softplus(-zf)
        lf_ref[...] = lf[:, :H]
        for j in range(n_sub):
            r0 = j * lb
            cv = lf[r0:r0 + lb]
            sh = 1
            while sh < lb:
                cv = cv + jnp.where(row >= sh, pltpu.roll(cv, sh, axis=0), 0.0)
                sh *= 2
            if carry:
                cv = cv + c_carry[...]
                c_carry[...] = cv[lb - 1:lb]
            c_ref[r0:r0 + lb, :] = cv[:, :H]
    else:
        krow_ref[...] = kn.astype(krow_ref.dtype)

        def proj_t(r0, n):
            return _mm(_wparts(wt_ref, slice(r0, r0 + n), slice(None)), u, "nt") + bt_ref[r0:r0 + n, :]

        zq_t = proj_t(0, C)
        msq = _mm(g64_ref[...], act(zq_t * zq_t))
        q_ref[0] = (zq_t * lax.rsqrt(msq + EPS) * (qgc_ref[...] * att_scale)).astype(q_ref.dtype)
        zk_t = proj_t(C, C)
        msk_t = _mm(g64_ref[...], act(zk_t * zk_t))
        k_ref[0] = zk_t * lax.rsqrt(msk_t + EPS) * kgc_ref[...]
        v_ref[0] = proj_t(2 * C, C)
        lf_t = -_softplus(-proj_t(3 * C, SUBLANES_BF16))
        lf_ref[0] = lf_t[:H]
        incl = (_iota((tm, tm), 0) <= _iota((tm, tm), 1)).astype(BF16)
        c_t = _dot_exact_rhs01(lf_t, incl)
        if carry:
            c_t = c_t + c_carry[...]
            c_carry[...] = c_carry[...] + jnp.sum(lf_t, axis=1, keepdims=True)
        c_ref[0] = c_t[:H]


def _block_diag(w, size):
    n, a, _ = w.shape
    per = size // a
    w = w.reshape(n // per, per, a, a)
    eye = jnp.eye(per, dtype=w.dtype)
    return jnp.einsum('gpij,pq->gpiqj', w, eye).reshape(n // per, size, size)


def _group_mean_matrix(c, width):
    idx = np.arange(c) // width
    return jnp.asarray((idx[:, None] == idx[None, :]).astype(np.float32) / width, dtype=BF16)


def _mix_call(x, lw, pool_prev, conv_prev, h_prev, *, seq_len, pos0, H, tm, transposed_kv):
    T, D = x.shape
    C = D // 2
    lb = min(tm, seq_len)
    tiles_per_seq = max(seq_len // tm, 1)
    n_sub = tm // lb
    nseq_blk = 1 if tiles_per_seq > 1 else n_sub
    assert T % tm == 0 and tm % lb == 0 and seq_len % lb == 0 and lb % 8 == 0
    assert not transposed_kv or tiles_per_seq > 1 or seq_len == tm
    n_tiles = T // tm
    nseq = pool_prev.shape[0]
    if tiles_per_seq > 1:
        smap = lambda i: (i // tiles_per_seq, 0, 0)
    else:
        smap = lambda i: (i, 0, 0)
    tok = lambda w: pl.BlockSpec((tm, w), lambda i: (i, 0))
    tr = lambda w: pl.BlockSpec((1, w, tm), lambda i: (i // tiles_per_seq, 0, i % tiles_per_seq))
    win = lw['w_a'].shape[2]
    wtr = lw['w_t'].shape[1]
    act_dtype = F32 if lw['w_a'].shape[0] > 1 else BF16
    kern = functools.partial(_mix_kernel, C=C, H=H, tm=tm, lb=lb, tiles_per_seq=tiles_per_seq, pos0=pos0,
                             att_scale=float((C // H) ** -0.5), mem_scale=float(lw['mem_head_dim'] ** -0.5),
                             transposed_kv=transposed_kv)
    in_specs = [tok(D), _const_spec((1, D)), _const_spec(lw['w_a'].shape), _const_spec((1, win)),
                _const_spec(lw['w_t'].shape), _const_spec((wtr, 1)),
                _const_spec(lw['pool_w'].shape), _const_spec((1, C)), _const_spec((CONV_WIDTH, C)),
                _const_spec((1, C)), _const_spec(lw['wa'].shape), _const_spec((1, C)),
                _const_spec(lw['wx'].shape), _const_spec((1, C)), _const_spec((1, C)),
                _const_spec((1, C)), _const_spec((C, 1)), _const_spec((1, C)), _const_spec((C, 1)),
                _const_spec((1, C)), _const_spec((C, C)), _const_spec((C, C)),
                pl.BlockSpec((nseq_blk, POOL_HIST, C), smap), pl.BlockSpec((nseq_blk, CONV_HIST, C), smap),
                pl.BlockSpec((nseq_blk, 1, C), smap)]
    if transposed_kv:
        kv_shape = lambda w, dt: jax.ShapeDtypeStruct((nseq, w, seq_len), dt)
        kv_spec = tr
        c_rows = SUBLANES_BF16
    else:
        kv_shape = lambda w, dt: jax.ShapeDtypeStruct((T, w), dt)
        kv_spec = tok
        c_rows = 1
    out_shape = [jax.ShapeDtypeStruct((T, C), act_dtype)] * 3 + [
        kv_shape(C, act_dtype), kv_shape(C, F32), kv_shape(C, F32), kv_shape(H, F32), kv_shape(H, F32)]
    out_specs = [tok(C)] * 3 + [kv_spec(C), kv_spec(C), kv_spec(C), kv_spec(H), kv_spec(H)]
    if transposed_kv:
        out_shape.append(jax.ShapeDtypeStruct((T, C), act_dtype))
        out_specs.append(tok(C))
    out_shape += [jax.ShapeDtypeStruct((nseq, POOL_HIST, C), F32), jax.ShapeDtypeStruct((nseq, CONV_HIST, C), F32),
                  jax.ShapeDtypeStruct((nseq, 1, C), F32)]
    out_specs += [pl.BlockSpec((nseq_blk, POOL_HIST, C), smap), pl.BlockSpec((nseq_blk, CONV_HIST, C), smap),
                  pl.BlockSpec((nseq_blk, 1, C), smap)]
    scratch = [pltpu.VMEM((tm, C), F32)] * 4 + [
        pltpu.VMEM((POOL_HIST, C), F32), pltpu.VMEM((CONV_HIST, C), F32), pltpu.VMEM((1, C), F32),
        pltpu.VMEM((c_rows, 1 if transposed_kv else LANES), F32)]
    return pl.pallas_call(
        kern, grid=(n_tiles,), in_specs=in_specs, out_specs=out_specs, out_shape=out_shape,
        scratch_shapes=scratch, compiler_params=_cparams(("arbitrary",)), name="mix",
    )(x, lw['norm_mix'], lw['w_a'], lw['b_a'], lw['w_t'], lw['b_t'], lw['pool_w'], lw['pool_scale'],
      lw['conv_w'], lw['conv_b'], lw['wa'], lw['ba'], lw['wx'], lw['bx'], lw['lam'], lw['q_gain'],
      lw['q_gain_col'], lw['k_gain'], lw['k_gain_col'], lw['mq_gain'], lw['g64'], lw['g128'],
      pool_prev, conv_prev, h_prev)


def _fox_prompt_kernel(qt_ref, k_ref, vt_ref, ccol_ref, o_ref, qs_s, s_s, p_s, m_s, l_s, acc_s, *, tq, hg, dh):
    qi = pl.program_id(2)
    width = hg * dh
    n_pieces = qs_s.shape[0]
    row_h = _iota((width, 1), 0) // dh
    for i, q in enumerate(_parts(qt_ref[0])):
        for h in range(hg):
            qs_s[i, :, h * tq:(h + 1) * tq] = jnp.where(row_h == h, q, jnp.zeros_like(q))
    m_s[...] = jnp.full_like(m_s, NEG_INF)
    l_s[...] = jnp.zeros_like(l_s)
    acc_s[...] = jnp.zeros_like(acc_s)
    kpos = _iota((tq, 1), 0)
    operand = (lambda v: v) if n_pieces > 1 else (lambda v: v.astype(BF16))

    def block(kb, masked):
        start = pl.multiple_of(kb * tq, tq)
        qs = tuple(qs_s[i] for i in range(n_pieces))
        s_s[...] = _mm(k_ref[pl.ds(start, tq), :], qs)
        cc = ccol_ref[0, 0, pl.ds(start, tq), :]
        for h in range(hg):
            bias = jnp.broadcast_to(cc[:, h:h + 1], (tq, LANES))
            for j in range(tq // LANES):
                lanes = slice(h * tq + j * LANES, h * tq + (j + 1) * LANES)
                s = s_s[:, lanes] - bias
                if masked:
                    s = jnp.where(j * LANES + _iota((1, LANES), 1) >= kpos, s, NEG_INF)
                m_old = m_s[:, lanes]
                m_new = jnp.maximum(m_old, jnp.max(s, axis=0, keepdims=True))
                alpha = jnp.exp(m_old - m_new)
                p = jnp.exp(s - m_new)
                l_s[:, lanes] = alpha * l_s[:, lanes] + jnp.sum(p, axis=0, keepdims=True)
                m_s[:, lanes] = m_new
                acc_rows = slice(h * dh, (h + 1) * dh)
                acc_lanes = slice(j * LANES, (j + 1) * LANES)
                acc_s[acc_rows, acc_lanes] = acc_s[acc_rows, acc_lanes] * alpha
                for i, piece in enumerate(_parts(operand(p))):
                    p_s[i, :, lanes] = piece
        v = operand(vt_ref[0, :, pl.ds(start, tq)])
        for h in range(hg):
            ps = tuple(p_s[i, :, h * tq:(h + 1) * tq] for i in range(n_pieces))
            acc_s[h * dh:(h + 1) * dh, :] += _mm(v[h * dh:(h + 1) * dh, :], ps)

    def body(kb, c):
        block(kb, False)
        return c

    lax.fori_loop(0, qi, body, 0)
    block(qi, True)
    inv_l = 1.0 / l_s[...]
    o_t = jnp.concatenate([acc_s[h * dh:(h + 1) * dh, :] * inv_l[:, h * tq:(h + 1) * tq] for h in range(hg)], axis=0)
    o_ref[...] = o_t.T.astype(o_ref.dtype)


def _fox_prompt_call(qt, k, vt, ct, *, tq):
    B, C, S = qt.shape
    T = B * S
    n_pieces = 2 if qt.dtype == F32 else 1
    H = ct.shape[1]
    dh = C // H
    hg = MXU_DIM // dh
    ng = H // hg
    nq = S // tq
    width = hg * dh
    assert tq % LANES == 0
    ccol = jnp.transpose(ct.reshape(B, ng, hg, S), (0, 1, 3, 2))
    kern = functools.partial(_fox_prompt_kernel, tq=tq, hg=hg, dh=dh)
    return pl.pallas_call(
        kern, grid=(B, ng, nq),
        in_specs=[pl.BlockSpec((1, width, tq), lambda b, g, i: (b, g, i)),
                  pl.BlockSpec((S, width), lambda b, g, i: (b, g)),
                  pl.BlockSpec((1, width, S), lambda b, g, i: (b, g, 0)),
                  pl.BlockSpec((1, 1, S, hg), lambda b, g, i: (b, g, 0, 0))],
        out_specs=pl.BlockSpec((tq, width), lambda b, g, i: (b * nq + i, g)),
        out_shape=jax.ShapeDtypeStruct((T, C), qt.dtype),
        scratch_shapes=[pltpu.VMEM((n_pieces, width, hg * tq), BF16), pltpu.VMEM((tq, hg * tq), F32),
                        pltpu.VMEM((n_pieces, tq, hg * tq), BF16), pltpu.VMEM((1, hg * tq), F32),
                        pltpu.VMEM((1, hg * tq), F32), pltpu.VMEM((width, tq), F32)],
        compiler_params=_cparams(("arbitrary", "arbitrary", "arbitrary")), name="fox_prompt",
    )(qt, k, vt, ccol)


def _fox_decode_kernel(pt_ref, q_ref, kn_ref, vn_ref, cn_ref, *rest, pp, H, dh, L, page):
    kt_refs = rest[:pp]
    vt_refs = rest[pp:2 * pp]
    lf_refs = rest[2 * pp:3 * pp]
    o_ref = rest[3 * pp]
    kb_s, vb_s, knp_s, vnp_s, m_s, l_s, acc_s, dsum_s = rest[3 * pp + 1:]
    j = pl.program_id(1)
    nj = pl.num_programs(1)
    C = H * dh
    R = L * H

    q = q_ref[0].astype(F32)
    head_mask = (_iota((1, C), 1) // dh) == _iota((H, 1), 0)
    qs = jnp.concatenate([jnp.where(head_mask, q[t:t + 1, :], 0.0) for t in range(L)], axis=0).astype(BF16)

    @pl.when(j == 0)
    def _():
        knp_s[...] = jnp.zeros_like(knp_s)
        vnp_s[...] = jnp.zeros_like(vnp_s)
        knp_s[0:L, :] = kn_ref[0]
        vnp_s[0:L, :] = vn_ref[0]
        s = _dot_nt(qs, knp_s[...].astype(BF16)) - cn_ref[0]
        s = jnp.where(_iota((R, LANES), 1) <= _iota((R, LANES), 0) // H, s, NEG_INF)
        m_new = jnp.max(s, axis=-1, keepdims=True)
        p = jnp.exp(s - m_new)
        m_s[...] = m_new
        l_s[...] = jnp.sum(p, axis=-1, keepdims=True)
        acc_s[...] = _dot(p.astype(BF16), vnp_s[...].astype(BF16))
        dsum_s[...] = jnp.zeros_like(dsum_s)

    for i in range(pp):
        kb_s[:, i * page:(i + 1) * page] = kt_refs[i][0].astype(BF16)
        vb_s[:, i * page:(i + 1) * page] = vt_refs[i][0].astype(BF16)
    lf_all = jnp.concatenate([lf_refs[i][0] for i in range(pp)], axis=0)
    later = (_iota((page, page), 0) > _iota((page, page), 1)).astype(BF16)
    suffix = _dot_exact_rhs01(lf_all, later)
    total = jnp.sum(lf_all, axis=-1, keepdims=True)
    dsum = dsum_s[...]
    parts = []
    for i in range(pp):
        d_i = suffix[i * H:(i + 1) * H] + dsum
        parts.append(jnp.concatenate([d_i] * L, axis=0))
        dsum = dsum + total[i * H:(i + 1) * H]
    dsum_s[...] = dsum
    d = jnp.concatenate(parts, axis=1) if pp > 1 else parts[0]
    s = _dot(qs, kb_s[...]) + d
    m_old = m_s[...]
    m_new = jnp.maximum(m_old, jnp.max(s, axis=-1, keepdims=True))
    alpha = jnp.exp(m_old - m_new)
    p = jnp.exp(s - m_new)
    l_s[...] = alpha * l_s[...] + jnp.sum(p, axis=-1, keepdims=True)
    m_s[...] = m_new
    acc_s[...] = alpha * acc_s[...] + _dot_nt(p.astype(BF16), vb_s[...])

    @pl.when(j == nj - 1)
    def _():
        o = acc_s[...] * (1.0 / l_s[...])
        keep = (_iota((1, C), 1) // dh) == (_iota((R, 1), 0) % H)
        om = jnp.where(keep, o, 0.0).astype(BF16)
        pick = (_iota((SUBLANES_BF16, R), 1) // H == _iota((SUBLANES_BF16, R), 0)).astype(BF16)
        o_ref[0] = _dot(pick, om)[0:L]


def _fox_decode_call(page_table, q, k_new, v_new, c_new, kt_pages, vt_pages, lf_pages, *, page_base, pp):
    Bs, L, C = q.shape
    H = c_new.shape[-1]
    dh = C // H
    n_pages = page_table.shape[1]
    page = kt_pages.shape[2]
    assert n_pages % pp == 0 and L * H <= LANES and L <= SUBLANES_BF16
    nj = n_pages // pp
    R = L * H
    cn = jnp.broadcast_to(jnp.transpose(c_new, (0, 2, 1))[:, None], (Bs, L, H, L)).reshape(Bs, R, L)
    cn = jnp.pad(cn, ((0, 0), (0, 0), (0, LANES - L)))
    seq = lambda w: pl.BlockSpec((1, L, w), lambda b, j, pt: (b, 0, 0))

    def page_spec(i, rows):
        return pl.BlockSpec((1, rows, page),
                            lambda b, j, pt: (page_base + pt[b, n_pages - 1 - (j * pp + i)], 0, 0))

    in_specs = ([seq(C), seq(C), seq(C), pl.BlockSpec((1, R, LANES), lambda b, j, pt: (b, 0, 0))]
                + [page_spec(i, C) for i in range(pp)] + [page_spec(i, C) for i in range(pp)]
                + [page_spec(i, H) for i in range(pp)])
    kern = functools.partial(_fox_decode_kernel, pp=pp, H=H, dh=dh, L=L, page=page)
    grid_spec = pltpu.PrefetchScalarGridSpec(
        num_scalar_prefetch=1, grid=(Bs, nj), in_specs=in_specs,
        out_specs=pl.BlockSpec((1, L, C), lambda b, j, pt: (b, 0, 0)),
        scratch_shapes=[pltpu.VMEM((C, pp * page), BF16), pltpu.VMEM((C, pp * page), BF16),
                        pltpu.VMEM((LANES, C), F32), pltpu.VMEM((LANES, C), F32),
                        pltpu.VMEM((R, 1), F32), pltpu.VMEM((R, 1), F32),
                        pltpu.VMEM((R, C), F32), pltpu.VMEM((H, 1), F32)])
    return pl.pallas_call(
        kern, grid_spec=grid_spec, out_shape=jax.ShapeDtypeStruct((Bs, L, C), F32),
        compiler_params=_cparams(("arbitrary", "arbitrary")), name="fox_decode",
    )(page_table, q, k_new, v_new, cn, *([kt_pages] * pp), *([vt_pages] * pp), *([lf_pages] * pp))


def _mem_kv_kernel(m_ref, g_ref, w_ref, kg_ref, g128_ref, k_ref, v_ref, kb_ref, vb_ref, *, C):
    hp = w_ref.shape[0] > 1
    act = (lambda v: v) if hp else (lambda v: v.astype(BF16))
    u = _parts(act(_rms(m_ref[...], g_ref[...])))
    zk = _mm(u, _wparts(w_ref, slice(None), slice(0, C)))
    ms = _mm(act(zk * zk), g128_ref[...])
    kn = zk * lax.rsqrt(ms + EPS) * kg_ref[...]
    k_ref[...] = kn
    kb_ref[...] = kn.astype(BF16)
    zv = _mm(u, _wparts(w_ref, slice(None), slice(C, 2 * C)))
    v_ref[...] = zv
    vb_ref[...] = zv.astype(BF16)


def _mem_kv_call(mem, g, w, k_gain, g128, *, tm):
    T, D = mem.shape
    C = w.shape[2] // 2
    tok = lambda wd: pl.BlockSpec((tm, wd), lambda i: (i, 0))
    return pl.pallas_call(
        functools.partial(_mem_kv_kernel, C=C), grid=(T // tm,),
        in_specs=[tok(D), _const_spec((1, D)), _const_spec(w.shape), _const_spec((1, C)), _const_spec((C, C))],
        out_specs=[tok(C)] * 4,
        out_shape=[jax.ShapeDtypeStruct((T, C), F32), jax.ShapeDtypeStruct((T, C), F32),
                   jax.ShapeDtypeStruct((T, C), BF16), jax.ShapeDtypeStruct((T, C), BF16)],
        compiler_params=_cparams(("arbitrary",)), name="mem_kv",
    )(mem, g, w, k_gain, g128)


def _mem_attn_kernel(q_ref, k_ref, v_ref, o_ref, *, heads, dh):
    q = q_ref[0]
    hp = q.dtype == F32
    outs = []
    for h in range(heads):
        sl = slice(h * dh, (h + 1) * dh)
        s = _mm(q[:, sl], k_ref[0, :, sl], "nt")
        p = jnp.exp(s - jnp.max(s, axis=-1, keepdims=True))
        l = jnp.sum(p, axis=-1, keepdims=True)
        outs.append(_mm(p if hp else p.astype(BF16), v_ref[0, :, sl]) * (1.0 / l))
    o_ref[0] = jnp.concatenate(outs, axis=1).astype(o_ref.dtype)


def _mem_attn_call(q, k, v, *, heads, tq):
    B, S, C = q.shape
    M = k.shape[1]
    tq = min(tq, S)
    return pl.pallas_call(
        functools.partial(_mem_attn_kernel, heads=heads, dh=C // heads), grid=(B, S // tq),
        in_specs=[pl.BlockSpec((1, tq, C), lambda b, i: (b, i, 0)),
                  pl.BlockSpec((1, M, C), lambda b, i: (b, 0, 0)),
                  pl.BlockSpec((1, M, C), lambda b, i: (b, 0, 0))],
        out_specs=pl.BlockSpec((1, tq, C), lambda b, i: (b, i, 0)),
        out_shape=jax.ShapeDtypeStruct((B, S, C), q.dtype),
        compiler_params=_cparams(("arbitrary", "arbitrary")), name="mem_attn",
    )(q, k, v)


def _sorted_rows(tm, n_exp):
    worst = 2 * tm + n_exp * (MOE_BLOCK - 1)
    return -(-worst // MXU_DIM) * MXU_DIM


def _pick_matrix(pos_row, rows):
    return jnp.where(_iota((rows, pos_row.shape[1]), 0) == pos_row.astype(jnp.int32), 1.0, 0.0).astype(BF16)


def _merge_kernel(x_ref, op_ref, ol_ref, oa_ref, om_ref, g_ref, wg_ref, bg_ref, wb_ref, wo_ref,
                  gf_ref, wr_ref, br_ref, x1_ref, xs_ref, pos_ref, run_ref,
                  *, D, n_groups, per_group, tm, rows):
    x = x_ref[...]
    hp = wb_ref.shape[0] > 1
    act = (lambda v: v) if hp else (lambda v: v.astype(BF16))
    u = _parts(act(_rms(x, g_ref[...])))
    merged = None
    for n, o_ref in enumerate((op_ref, ol_ref, oa_ref, om_ref)):
        cols = slice(n * D, (n + 1) * D)
        gate = jax.nn.sigmoid(_mm(u, _wparts(wg_ref, slice(None), cols)) + bg_ref[:, cols])
        term = gate * _mm(o_ref[...], _wparts(wb_ref, n))
        merged = term if merged is None else merged + term
    x1 = x + _mm(act(merged), _wparts(wo_ref))
    x1_ref[...] = x1
    t = _rms(x1, gf_ref[...])
    tb = t.astype(BF16)
    t_lo = (t - tb.astype(F32)).astype(BF16)
    w_hi = wr_ref[0]
    w_lo = wr_ref[1]
    logit = _dot_nt(w_hi, tb) + _dot_nt(w_hi, t_lo) + _dot_nt(w_lo, tb) + br_ref[...]
    n_exp = n_groups * per_group
    rid = _iota(logit.shape, 0)
    big = jnp.int32(1 << 20)
    is_g = (rid >= n_exp) & (rid < n_exp + n_groups)
    gl = jnp.where(is_g, logit, NEG_INF)
    gmax = jnp.max(gl, axis=0, keepdims=True)
    gidx = jnp.min(jnp.where(gl == gmax, rid, big), axis=0, keepdims=True) - n_exp
    g_w = 1.0 / jnp.sum(jnp.where(is_g, jnp.exp(gl - gmax), 0.0), axis=0, keepdims=True)
    in_grp = (rid >= gidx * per_group) & (rid < (gidx + 1) * per_group)
    el = jnp.where(in_grp, logit, NEG_INF)
    v1 = jnp.max(el, axis=0, keepdims=True)
    i1 = jnp.min(jnp.where(el == v1, rid, big), axis=0, keepdims=True)
    el2 = jnp.where(rid == i1, NEG_INF, el)
    v2 = jnp.max(el2, axis=0, keepdims=True)
    i2 = jnp.min(jnp.where(el2 == v2, rid, big), axis=0, keepdims=True)
    e21 = jnp.exp(v2 - v1)
    w1 = g_w / (1.0 + e21)
    w2 = g_w * e21 / (1.0 + e21)

    sel1 = rid == i1
    sel2 = rid == i2
    member = jnp.where(sel1 | sel2, 1.0, 0.0)
    count = jnp.sum(member, axis=1, keepdims=True)
    padded = jnp.floor((count + (MOE_BLOCK - 1)) * (1.0 / MOE_BLOCK)) * MOE_BLOCK
    below = jnp.where(_iota((LANES, LANES), 1) < _iota((LANES, LANES), 0), 1.0, 0.0).astype(BF16)
    start = _dot(below, jnp.broadcast_to(padded, (LANES, LANES)).astype(BF16))[:, 0:1]
    earlier = jnp.where(_iota((tm, tm), 0) < _iota((tm, tm), 1), 1.0, 0.0).astype(BF16)
    rank = _dot(member.astype(BF16), earlier)
    where_to = start + rank
    pos1 = jnp.sum(jnp.where(sel1, where_to, 0.0), axis=0, keepdims=True)
    pos2 = jnp.sum(jnp.where(sel2, where_to, 0.0), axis=0, keepdims=True)
    pick1 = _pick_matrix(pos1, rows)
    pick2 = _pick_matrix(pos2, rows)
    xs_ref[:, 0:D] = _dot(pick1 + pick2, tb).astype(BF16)
    r8 = _iota((LANES, tm), 0)

    def pieces(w):
        hi, mid, lo = _split3(w)
        return jnp.where(r8 == 0, hi.astype(F32), jnp.where(r8 == 1, mid.astype(F32),
                         jnp.where(r8 == 2, lo.astype(F32), 0.0))).astype(BF16)

    xs_ref[:, D:D + LANES] = (_dot_nt(pick1, pieces(w1)) + _dot_nt(pick2, pieces(w2))).astype(BF16)
    r8p = _iota((8, tm), 0)
    pos_ref[0] = jnp.where(r8p == 0, pos1, jnp.where(r8p == 1, pos2, 0.0))
    eye = _iota((LANES, LANES), 0) == _iota((LANES, LANES), 1)
    to_row = lambda col: jnp.sum(jnp.where(eye, col, 0.0), axis=0, keepdims=True)
    r8r = _iota((8, LANES), 0)
    run_ref[0] = jnp.where(r8r == 0, to_row(padded), jnp.where(r8r == 1, to_row(start), 0.0))


def _merge_call(x, o_pool, o_lru, o_att, o_mem, lw, *, tm, n_groups, per_group):
    T, D = x.shape
    C = D // 2
    n_tiles = T // tm
    rows = _sorted_rows(tm, n_groups * per_group)
    tok = lambda w: pl.BlockSpec((tm, w), lambda i: (i, 0))
    kern = functools.partial(_merge_kernel, D=D, n_groups=n_groups, per_group=per_group, tm=tm, rows=rows)
    return pl.pallas_call(
        kern, grid=(n_tiles,),
        in_specs=[tok(D), tok(C), tok(C), tok(C), tok(C), _const_spec((1, D)), _const_spec(lw['w_g'].shape),
                  _const_spec((1, 4 * D)), _const_spec(lw['w_branch'].shape), _const_spec(lw['w_out'].shape),
                  _const_spec((1, D)),
                  _const_spec((2, LANES, D)), _const_spec((LANES, 1))],
        out_specs=[tok(D), pl.BlockSpec((rows, D + LANES), lambda i: (i, 0)),
                   pl.BlockSpec((1, 8, tm), lambda i: (i, 0, 0)), pl.BlockSpec((1, 8, LANES), lambda i: (i, 0, 0))],
        out_shape=[jax.ShapeDtypeStruct((T, D), F32), jax.ShapeDtypeStruct((n_tiles * rows, D + LANES), BF16),
                   jax.ShapeDtypeStruct((n_tiles, 8, tm), F32), jax.ShapeDtypeStruct((n_tiles, 8, LANES), F32)],
        compiler_params=_cparams(("arbitrary",)), name="merge",
    )(x, o_pool, o_lru, o_att, o_mem, lw['norm_mix'], lw['w_g'], lw['b_g'], lw['w_branch'], lw['w_out'],
      lw['norm_ffn'], lw['w_route'], lw['b_route'])


def _block_tables(run_info, *, n_exp, rows):
    n_tiles = run_info.shape[0]
    nblk = (run_info[:, 0, :n_exp] / MOE_BLOCK).astype(jnp.int32).T
    first = (run_info[:, 1, :n_exp] / MOE_BLOCK).astype(jnp.int32).T + \
        (jnp.arange(n_tiles, dtype=jnp.int32) * (rows // MOE_BLOCK))[None, :]
    nb = MOE_CHUNK_BLOCKS
    run_end = jnp.cumsum(nblk, axis=1)
    per_expert = run_end[:, -1]
    chunks_e = (per_expert + nb - 1) // nb
    chunk_end = jnp.cumsum(chunks_e)
    n_chunks = chunk_end[-1]
    max_chunks = n_tiles * (rows // MOE_BLOCK) // nb + n_exp
    cid = jnp.arange(max_chunks, dtype=jnp.int32)
    experts = jnp.arange(n_exp, dtype=jnp.int32)
    chunk_expert = jnp.sum((chunk_end[None, :] <= cid[:, None]).astype(jnp.int32), axis=1)
    last_real = jnp.max(jnp.where(chunks_e > 0, experts, 0))
    chunk_expert = jnp.where(cid < n_chunks, chunk_expert, last_real)
    mine = chunk_expert[:, None] == experts[None, :]

    def of_chunk(per_e):
        hot = mine.reshape(mine.shape + (1,) * (per_e.ndim - 1))
        return jnp.sum(jnp.where(hot, per_e[None], 0), axis=1)

    q = ((cid - of_chunk(chunk_end - chunks_e)) * nb)[:, None] + jnp.arange(nb, dtype=jnp.int32)[None, :]
    valid = (cid < n_chunks)[:, None] & (q < of_chunk(per_expert)[:, None])
    ends = of_chunk(run_end)[:, None, :]
    done = ends <= q[:, :, None]
    tile = jnp.minimum(jnp.sum(done.astype(jnp.int32), axis=2), n_tiles - 1)
    before = jnp.max(jnp.where(done, ends, 0), axis=2)
    tiles = jnp.arange(n_tiles, dtype=jnp.int32)[None, None, :]
    run_first = jnp.sum(jnp.where(tiles == tile[:, :, None], of_chunk(first)[:, None, :], 0), axis=2)
    src = jnp.where(valid, run_first + (q - before), -1).astype(jnp.int32).reshape(-1)
    return chunk_expert.astype(jnp.int32), src, n_chunks.reshape(1).astype(jnp.int32)


def _expert_kernel(ce_ref, src_ref, nc_ref, xs_in, wgu_ref, wd_ref, xs_out, xbuf, ybuf, in_sem, out_sem,
                   *, D, F):
    del xs_in
    c = pl.program_id(0)
    n = nc_ref[0]
    nb = MOE_CHUNK_BLOCKS
    blk = MOE_BLOCK

    def in_copy(chunk, slot, k):
        row = pl.multiple_of(src_ref[chunk * nb + k] * blk, blk)
        return pltpu.make_async_copy(xs_out.at[pl.ds(row, blk), :], xbuf.at[slot, pl.ds(k * blk, blk), :],
                                     in_sem.at[slot])

    def out_copy(chunk, slot, k):
        row = pl.multiple_of(src_ref[chunk * nb + k] * blk, blk)
        return pltpu.make_async_copy(ybuf.at[slot, pl.ds(k * blk, blk), :],
                                     xs_out.at[pl.ds(row, blk), pl.ds(0, D)], out_sem.at[slot])

    def for_valid(chunk, fn):
        for k in range(nb):
            @pl.when(src_ref[chunk * nb + k] >= 0)
            def _():
                fn(k)

    @pl.when(c == 0)
    def _():
        xbuf[...] = jnp.zeros_like(xbuf)
        for_valid(0, lambda k: in_copy(0, 0, k).start())

    @pl.when(c < n)
    def _():
        slot = c % 2
        for_valid(c, lambda k: in_copy(c, slot, k).wait())

        @pl.when(c + 1 < n)
        def _():
            for_valid(c + 1, lambda k: in_copy(c + 1, 1 - slot, k).start())

        @pl.when(c >= 2)
        def _():
            for_valid(c - 2, lambda k: out_copy(c - 2, slot, k).wait())

        xin = xbuf[slot]
        x = xin[:, 0:D]
        wp = xin[:, D:D + LANES].astype(F32)
        w = wp[:, 0:1] + wp[:, 1:2] + wp[:, 2:3]
        gu = _dot(x, wgu_ref[0])
        h = _silu(gu[:, :F]) * gu[:, F:]
        ybuf[slot] = _dot((h * w).astype(BF16), wd_ref[0]).astype(BF16)
        for_valid(c, lambda k: out_copy(c, slot, k).start())

        @pl.when(c == n - 1)
        def _():
            for_valid(c, lambda k: out_copy(c, slot, k).wait())

            @pl.when(c >= 1)
            def _():
                for_valid(c - 1, lambda k: out_copy(c - 1, 1 - slot, k).wait())


def _expert_call(xs, chunk_expert, src, n_chunks, wgu, wd):
    n_rows, width = xs.shape
    E, D, F2 = wgu.shape
    F = F2 // 2
    max_chunks = chunk_expert.shape[0]
    chunk_rows = MOE_CHUNK_BLOCKS * MOE_BLOCK
    grid_spec = pltpu.PrefetchScalarGridSpec(
        num_scalar_prefetch=3, grid=(max_chunks,),
        in_specs=[pl.BlockSpec(memory_space=pl.ANY),
                  pl.BlockSpec((1, D, F2), lambda c, ce, src, nc: (ce[c], 0, 0)),
                  pl.BlockSpec((1, F, D), lambda c, ce, src, nc: (ce[c], 0, 0))],
        out_specs=pl.BlockSpec(memory_space=pl.ANY),
        scratch_shapes=[pltpu.VMEM((2, chunk_rows, width), BF16), pltpu.VMEM((2, chunk_rows, D), BF16),
                        pltpu.SemaphoreType.DMA((2,)), pltpu.SemaphoreType.DMA((2,))])
    return pl.pallas_call(
        functools.partial(_expert_kernel, D=D, F=F), grid_spec=grid_spec,
        out_shape=jax.ShapeDtypeStruct((n_rows, width), BF16),
        input_output_aliases={3: 0},
        compiler_params=_cparams(("arbitrary",)), name="experts",
    )(chunk_expert, src, n_chunks, xs, wgu, wd)


def _combine_kernel(x1_ref, ys_ref, pos_ref, y_ref, *, D, rows):
    pos = pos_ref[0]
    pick = _pick_matrix(pos[0:1, :], rows) + _pick_matrix(pos[1:2, :], rows)
    y_ref[...] = x1_ref[...] + _dot_tn(pick, ys_ref[:, 0:D])


def _combine_call(x1, ys, pos, *, tm, rows):
    T, D = x1.shape
    width = ys.shape[1]
    return pl.pallas_call(
        functools.partial(_combine_kernel, D=D, rows=rows), grid=(T // tm,),
        in_specs=[pl.BlockSpec((tm, D), lambda i: (i, 0)), pl.BlockSpec((rows, width), lambda i: (i, 0)),
                  pl.BlockSpec((1, 8, tm), lambda i: (i, 0, 0))],
        out_specs=pl.BlockSpec((tm, D), lambda i: (i, 0)),
        out_shape=jax.ShapeDtypeStruct((T, D), F32),
        compiler_params=_cparams(("arbitrary",)), name="combine",
    )(x1, ys, pos)


PIECED_WEIGHTS = ('w_a', 'w_t', 'pool_w', 'wa', 'wx', 'w_mem_kv', 'w_g', 'w_branch', 'w_out')


def _single_pass(lw):
    return {k: (v[:1] if k in PIECED_WEIGHTS else v) for k, v in lw.items()}


def _prep_layer(l, p, *, C, H, mem_heads, high_precision):
    D = 2 * C
    pieces = functools.partial(_stack_parts, high_precision=high_precision)
    w_in, b_in = p['w_in'][l], p['b_in'][l]
    f0 = 6 * C
    w_a = pieces(jnp.concatenate([w_in[:, :f0], w_in[:, f0 + H:f0 + H + C], w_in[:, f0:f0 + H],
                                  jnp.zeros((D, LANES - H), F32)], axis=1))
    b_a = jnp.concatenate([b_in[:f0], b_in[f0 + H:f0 + H + C], b_in[f0:f0 + H], jnp.zeros((LANES - H,), F32)])[None]
    w_t = pieces(jnp.concatenate([w_in[:, 3 * C:6 * C].T, w_in[:, f0:f0 + H].T,
                                  jnp.zeros((SUBLANES_BF16 - H, D), F32)], axis=0))
    b_t = jnp.concatenate([b_in[3 * C:6 * C], b_in[f0:f0 + H], jnp.zeros((SUBLANES_BF16 - H,), F32)])[:, None]
    g0 = f0 + H + C
    n_exp = p['w_router'].shape[2]
    n_grp = p['w_group'].shape[2]
    w_r = jnp.concatenate([p['w_router'][l].T, p['w_group'][l].T,
                           jnp.zeros((LANES - n_exp - n_grp, D), F32)], axis=0)
    w_r_hi, w_r_lo = _stack_parts(w_r, True)
    b_r = jnp.concatenate([p['b_router'][l], p['b_group'][l], jnp.zeros((LANES - n_exp - n_grp,), F32)])[:, None]
    wg = p['w_gate'][l].reshape((n_exp,) + p['w_gate'].shape[3:])
    wu = p['w_up'][l].reshape((n_exp,) + p['w_up'].shape[3:])
    wd = p['w_down'][l].reshape((n_exp,) + p['w_down'].shape[3:])
    dh = C // H
    mdh = C // mem_heads
    k_gain = jnp.tile(p['k_norm'][l], H)
    q_gain = jnp.tile(p['q_norm'][l], H)
    return dict(
        mem_head_dim=mdh,
        norm_mix=p['norm_mix'][l][None], w_a=w_a, b_a=b_a, w_t=w_t, b_t=b_t,
        pool_w=pieces(_block_diag(p['pool_w'][l], MXU_DIM)), pool_scale=p['pool_scale'][l][None],
        conv_w=p['conv_w'][l], conv_b=p['conv_b'][l][None],
        wa=pieces(_block_diag(p['lru_wa'][l], MXU_DIM)), ba=p['lru_ba'][l][None],
        wx=pieces(_block_diag(p['lru_wx'][l], MXU_DIM)), bx=p['lru_bx'][l][None],
        lam=p['lru_lambda'][l][None],
        q_gain=q_gain[None], q_gain_col=q_gain[:, None], k_gain=k_gain[None], k_gain_col=k_gain[:, None],
        mq_gain=jnp.tile(p['mq_norm'][l], mem_heads)[None], mk_gain=jnp.tile(p['mk_norm'][l], mem_heads)[None],
        g64=_group_mean_matrix(C, dh), g128=_group_mean_matrix(C, mdh),
        mem_norm=p['mem_norm'][l][None], w_mem_kv=pieces(p['w_mem_kv'][l]),
        w_g=pieces(w_in[:, g0:]), b_g=b_in[g0:][None],
        w_branch=pieces(p['w_branch'][l]), w_out=pieces(p['w_out'][l]),
        norm_ffn=p['norm_ffn'][l][None], w_route=jnp.stack([w_r_hi, w_r_lo]), b_route=b_r,
        w_gu=jnp.concatenate([wg, wu], axis=2).astype(BF16), w_d=wd.astype(BF16),
        n_groups=n_grp, per_group=n_exp // n_grp,
    )


def _tail(x, o_pool, o_lru, o_att, o_mem, lw, *, tm):
    n_exp = lw['n_groups'] * lw['per_group']
    rows = _sorted_rows(tm, n_exp)
    x1, xs, pos, run_info = _merge_call(x, o_pool, o_lru, o_att, o_mem, lw, tm=tm,
                                        n_groups=lw['n_groups'], per_group=lw['per_group'])
    chunk_expert, src, n_chunks = _block_tables(run_info, n_exp=n_exp, rows=rows)
    ys = _expert_call(xs, chunk_expert, src, n_chunks, lw['w_gu'], lw['w_d'])
    return _combine_call(x1, ys, pos, tm=tm, rows=rows)


def kernel(x_prompt, x_sample, cache_k, cache_v, cache_logf, cache_mem_k, cache_mem_v, state_pool, state_conv,
           state_lru, page_table, mem_prompt, norm_mix, w_in, b_in, pool_w, pool_scale, conv_w, conv_b, lru_wa,
           lru_ba, lru_wx, lru_bx, lru_lambda, q_norm, k_norm, mq_norm, mem_norm, w_mem_kv, mk_norm, w_branch,
           w_out, norm_ffn, w_group, b_group, w_router, b_router, w_gate, w_up, w_down):
    p = dict(norm_mix=norm_mix, w_in=w_in, b_in=b_in, pool_w=pool_w, pool_scale=pool_scale, conv_w=conv_w,
             conv_b=conv_b, lru_wa=lru_wa, lru_ba=lru_ba, lru_wx=lru_wx, lru_bx=lru_bx, lru_lambda=lru_lambda,
             q_norm=q_norm, k_norm=k_norm, mq_norm=mq_norm, mem_norm=mem_norm, w_mem_kv=w_mem_kv, mk_norm=mk_norm,
             w_branch=w_branch, w_out=w_out, norm_ffn=norm_ffn, w_group=w_group, b_group=b_group,
             w_router=w_router, b_router=b_router, w_gate=w_gate, w_up=w_up, w_down=w_down)
    depth = w_in.shape[0]
    B, S, D = x_prompt.shape
    Bs, L, _ = x_sample.shape
    C = D // 2
    _, n_pool, page, H, dh = cache_k.shape
    M, mem_heads, mdh = cache_mem_k.shape[2:]
    n_pages = page_table.shape[1]
    past_len = n_pages * page
    n_pool_rows = state_pool.shape[2]
    n_conv_rows = state_conv.shape[2]
    Tp, Ts = B * S, Bs * L
    tm = TOKEN_TILE
    tm_s = min(tm, Ts)

    xp = x_prompt.reshape(Tp, D)
    xs = x_sample.reshape(Ts, D)
    memp = mem_prompt.reshape(B * M, D)
    pool0 = jnp.zeros((B, POOL_HIST, C), F32)
    conv0 = jnp.zeros((B, CONV_HIST, C), F32)
    h0 = jnp.zeros((B, 1, C), F32)
    kt_pages = jnp.transpose(cache_k, (0, 1, 3, 4, 2)).reshape(depth * n_pool, C, page)
    vt_pages = jnp.transpose(cache_v, (0, 1, 3, 4, 2)).reshape(depth * n_pool, C, page)
    lf_pages = jnp.transpose(cache_logf, (0, 1, 3, 2)).reshape(depth * n_pool, H, page)

    outs = {n: [] for n in ('kp', 'vp', 'lfp', 'mkp', 'mvp', 'poolp', 'convp', 'lrup',
                            'ks', 'vs', 'lfs', 'pools', 'convs', 'lrus')}
    for l in range(depth):
        hp = l < depth - 1
        lw = _prep_layer(l, p, C=C, H=H, mem_heads=mem_heads, high_precision=hp)
        lw_s = _single_pass(lw)
        mk, mv, mkb, mvb = _mem_kv_call(memp, lw['mem_norm'], lw['w_mem_kv'], lw['mk_gain'], lw['g128'], tm=512)
        (o_pool, o_lru, mq, qt, kt, vt, lft, ct, k_rows, pool_n, conv_n, h_n) = _mix_call(
            xp, lw, pool0, conv0, h0, seq_len=S, pos0=0, H=H, tm=tm, transposed_kv=True)
        o_att = _fox_prompt_call(qt, k_rows, vt, ct, tq=256)
        o_mem = _mem_attn_call(mq.reshape(B, S, C), (mk if hp else mkb).reshape(B, M, C),
                               (mv if hp else mvb).reshape(B, M, C), heads=mem_heads, tq=512).reshape(Tp, C)
        xp = _tail(xp, o_pool, o_lru, o_att, o_mem, lw, tm=tm)
        outs['kp'].append(jnp.transpose(kt.reshape(B, H, dh, S), (0, 3, 1, 2)))
        outs['vp'].append(jnp.transpose(vt.reshape(B, H, dh, S), (0, 3, 1, 2)))
        outs['lfp'].append(jnp.transpose(lft, (0, 2, 1)))
        outs['mkp'].append(mk.reshape(B, M, mem_heads, mdh)); outs['mvp'].append(mv.reshape(B, M, mem_heads, mdh))
        outs['poolp'].append(pool_n[:, POOL_HIST - n_pool_rows:]); outs['convp'].append(conv_n[:, CONV_HIST - n_conv_rows:])
        outs['lrup'].append(h_n[:, 0])
        pool_prev = jnp.pad(state_pool[l], ((0, 0), (POOL_HIST - n_pool_rows, 0), (0, 0)))
        conv_prev = jnp.pad(state_conv[l], ((0, 0), (CONV_HIST - n_conv_rows, 0), (0, 0)))
        (o_pool, o_lru, mq, q, k, v, lf, c, pool_n, conv_n, h_n) = _mix_call(
            xs, lw_s, pool_prev, conv_prev, state_lru[l][:, None], seq_len=L, pos0=past_len, H=H, tm=tm_s,
            transposed_kv=False)
        o_att = _fox_decode_call(page_table, q.reshape(Bs, L, C), k.reshape(Bs, L, C), v.reshape(Bs, L, C),
                                 c.reshape(Bs, L, H), kt_pages, vt_pages, lf_pages,
                                 page_base=l * n_pool,
                                 pp=math.gcd(n_pages, DECODE_PAGES_PER_STEP)).reshape(Ts, C).astype(BF16)
        o_mem = _mem_attn_call(mq.reshape(Bs, L, C), cache_mem_k[l].reshape(Bs, M, C).astype(BF16),
                               cache_mem_v[l].reshape(Bs, M, C).astype(BF16), heads=mem_heads, tq=L).reshape(Ts, C)
        xs = _tail(xs, o_pool, o_lru, o_att, o_mem, lw_s, tm=tm_s)
        outs['ks'].append(k.reshape(Bs, L, H, dh)); outs['vs'].append(v.reshape(Bs, L, H, dh))
        outs['lfs'].append(lf.reshape(Bs, L, H))
        outs['pools'].append(pool_n[:, POOL_HIST - n_pool_rows:]); outs['convs'].append(conv_n[:, CONV_HIST - n_conv_rows:])
        outs['lrus'].append(h_n[:, 0])

    st = lambda n: jnp.stack(outs[n])
    return (xp.reshape(B, S, D), xs.reshape(Bs, L, D),
            st('kp'), st('vp'), st('lfp'), st('mkp'), st('mvp'), st('poolp'), st('convp'), st('lrup'),
            st('ks'), st('vs'), st('lfs'), st('pools'), st('convs'), st('lrus'))
```

```python
import functools
import math

import jax
import jax.numpy as jnp
import numpy as np
from jax import lax
from jax.experimental import pallas as pl
from jax.experimental.pallas import tpu as pltpu

F32 = jnp.float32
BF16 = jnp.bfloat16

EPS = 1e-6
LRU_C = 8.0
POOL_WINDOWS = (2, 4, 8, 16)
POOL_HIST = 16
CONV_WIDTH = 4
CONV_HIST = 8
LANES = 128
SUBLANES_BF16 = 16
MXU_DIM = 256
VMEM_LIMIT = 56 * 1024 * 1024
TOKEN_TILE = 256
MOE_BLOCK = SUBLANES_BF16
MOE_CHUNK_BLOCKS = 32
DECODE_PAGES_PER_STEP = 32

NEG_INF = float("-inf")


def _dot(a, b):
    return jnp.dot(a, b, preferred_element_type=F32)


def _dot_nt(a, b):
    return lax.dot_general(a, b, (((1,), (1,)), ((), ())), preferred_element_type=F32)


def _dot_tn(a, b):
    return lax.dot_general(a, b, (((0,), (0,)), ((), ())), preferred_element_type=F32)


def _split3(x):
    hi = x.astype(BF16)
    r1 = x - hi.astype(F32)
    mid = r1.astype(BF16)
    lo = (r1 - mid.astype(F32)).astype(BF16)
    return hi, mid, lo


def _dot_exact_rhs01(x, m01):
    hi, mid, lo = _split3(x)
    return _dot(hi, m01) + _dot(mid, m01) + _dot(lo, m01)


def _parts(x):
    if isinstance(x, tuple):
        return x
    if x.dtype == BF16:
        return (x,)
    hi = x.astype(BF16)
    return hi, (x - hi.astype(F32)).astype(BF16)


def _mm(a, b, kind="nn"):
    pa, pb = _parts(a), _parts(b)
    f = {"nn": _dot, "nt": _dot_nt, "tn": _dot_tn}[kind]
    out = f(pa[0], pb[0])
    if len(pa) > 1:
        out = out + f(pa[1], pb[0])
    if len(pb) > 1:
        out = out + f(pa[0], pb[1])
    return out


def _wparts(ref, *idx):
    return tuple(ref[(p,) + idx] for p in range(ref.shape[0]))


def _round_to_bf16(w):
    return lax.reduce_precision(w, exponent_bits=8, mantissa_bits=7)


def _stack_parts(w, high_precision):
    hi = _round_to_bf16(w)
    if not high_precision:
        return hi.astype(BF16)[None]
    return jnp.stack([hi.astype(BF16), (w - hi).astype(BF16)])


def _softplus(y):
    return jnp.maximum(y, 0.0) + jnp.log1p(jnp.exp(-jnp.abs(y)))


def _gelu_tanh(x):
    return x * (0.5 * (1.0 + jnp.tanh(math.sqrt(2.0 / math.pi) * (x + 0.044715 * (x * x * x)))))


def _silu(x):
    return x * jax.nn.sigmoid(x)


def _rms(x, gain):
    ms = jnp.mean(x * x, axis=-1, keepdims=True)
    return x * lax.rsqrt(ms + EPS) * gain


def _iota(shape, dim):
    return lax.broadcasted_iota(jnp.int32, shape, dim)


def _cparams(sem):
    return pltpu.CompilerParams(dimension_semantics=sem, vmem_limit_bytes=VMEM_LIMIT)


def _const_spec(shape):
    nd = len(shape)
    return pl.BlockSpec(shape, lambda *_: (0,) * nd, pipeline_mode=pl.Buffered(1))


def _mix_kernel(x_ref, g_ref, w_ref, b_ref, wt_ref, bt_ref, poolw_ref, pscale_ref, convw_ref, convb_ref,
                wa_ref, ba_ref, wx_ref, bx_ref, lam_ref, qg_ref, qgc_ref, kg_ref, kgc_ref, mqg_ref,
                g64_ref, g128_ref, poolprev_ref, convprev_ref, hprev_ref,
                opool_ref, olru_ref, mq_ref, q_ref, k_ref, v_ref, lf_ref, c_ref, *rest,
                C, H, tm, lb, tiles_per_seq, pos0, att_scale, mem_scale, transposed_kv):
    if transposed_kv:
        krow_ref, *rest = rest
    (poolnew_ref, convnew_ref, hnew_ref, d_s, xc_s, a_s, b_s, pool_carry, conv_carry, h_carry, c_carry) = rest
    n_sub = tm // lb
    carry = tiles_per_seq > 1
    i = pl.program_id(0)
    gw = C // len(POOL_WINDOWS)

    hp = w_ref.shape[0] > 1
    act = (lambda v: v) if hp else (lambda v: v.astype(BF16))
    u = _parts(act(_rms(x_ref[...], g_ref[...])))

    def proj(n, width=C):
        cols = slice(n * C, n * C + width)
        return _mm(u, _wparts(w_ref, slice(None), cols)) + b_ref[:, cols]

    if carry:
        seq_off = (i % tiles_per_seq) * tm

        @pl.when(i % tiles_per_seq == 0)
        def _():
            pool_carry[...] = poolprev_ref[0]
            conv_carry[...] = convprev_ref[0]
            h_carry[...] = hprev_ref[0]
            c_carry[...] = jnp.zeros_like(c_carry)
    else:
        seq_off = 0

    row = _iota((lb, 1), 0)
    pos = pos0 + seq_off + row

    zp = proj(0)
    inv_cnt = [1.0 / jnp.minimum(w, pos + 1).astype(F32) for w in POOL_WINDOWS]
    for j in range(n_sub):
        r0 = j * lb
        prev = pool_carry[...] if carry else poolprev_ref[j]
        cur = zp[r0:r0 + lb]
        ext = jnp.concatenate([prev, cur], axis=0)
        for g, w in enumerate(POOL_WINDOWS):
            s = ext[:, g * gw:(g + 1) * gw]
            sh = 1
            while sh < w:
                s = s + pltpu.roll(s, sh, axis=0)
                sh *= 2
            d_s[r0:r0 + lb, g * gw:(g + 1) * gw] = s[POOL_HIST:] * inv_cnt[g] - cur[:, g * gw:(g + 1) * gw]
        new_state = ext[lb:lb + POOL_HIST]
        if carry:
            pool_carry[...] = new_state
            poolnew_ref[0] = new_state
        else:
            poolnew_ref[j] = new_state
    d = act(d_s[...])
    y_pool = jnp.concatenate(
        [_mm(d[:, n * MXU_DIM:(n + 1) * MXU_DIM], _wparts(poolw_ref, n)) for n in range(C // MXU_DIM)], axis=1)
    opool_ref[...] = (y_pool * pscale_ref[...]).astype(opool_ref.dtype)

    zx = proj(1)
    for j in range(n_sub):
        r0 = j * lb
        prev = conv_carry[...] if carry else convprev_ref[j]
        ext = jnp.concatenate([prev, zx[r0:r0 + lb]], axis=0)
        xc = convb_ref[...]
        for t in range(CONV_WIDTH):
            sh = CONV_WIDTH - 1 - t
            e = ext if sh == 0 else pltpu.roll(ext, sh, axis=0)
            xc = xc + e[CONV_HIST:] * convw_ref[t:t + 1, :]
        xc_s[r0:r0 + lb, :] = xc
        new_state = ext[lb:lb + CONV_HIST]
        if carry:
            conv_carry[...] = new_state
            convnew_ref[0] = new_state
        else:
            convnew_ref[j] = new_state

    xc = xc_s[...]
    xcb = act(xc)
    nb = C // MXU_DIM
    ra = jnp.concatenate([_mm(xcb[:, n * MXU_DIM:(n + 1) * MXU_DIM], _wparts(wa_ref, n)) for n in range(nb)], axis=1)
    rx = jnp.concatenate([_mm(xcb[:, n * MXU_DIM:(n + 1) * MXU_DIM], _wparts(wx_ref, n)) for n in range(nb)], axis=1)
    r = jax.nn.sigmoid(ra + ba_ref[...])
    ig = jax.nn.sigmoid(rx + bx_ref[...])
    log_a = (-LRU_C) * r * _softplus(-lam_ref[...])
    a = jnp.exp(log_a)
    th = jnp.tanh(log_a)
    one_minus_a2 = (-2.0) * th / (1.0 - th)
    mult = jnp.sqrt(one_minus_a2)
    pos_tile = pos0 + seq_off + _iota((tm, 1), 0) % lb
    mult = jnp.where(pos_tile == 0, 1.0, mult)
    a_s[...] = a
    b_s[...] = mult * ig * xc

    zg = proj(2)
    for j in range(n_sub):
        r0 = j * lb
        av = a_s[r0:r0 + lb, :]
        bv = b_s[r0:r0 + lb, :]
        sh = 1
        while sh < lb:
            m = row >= sh
            ar = pltpu.roll(av, sh, axis=0)
            br = pltpu.roll(bv, sh, axis=0)
            bv = jnp.where(m, av * br + bv, bv)
            av = jnp.where(m, av * ar, av)
            sh *= 2
        hp = h_carry[...] if carry else hprev_ref[j]
        h = av * hp + bv
        olru_ref[r0:r0 + lb, :] = (h * _gelu_tanh(zg[r0:r0 + lb])).astype(olru_ref.dtype)
        if carry:
            h_carry[...] = h[lb - 1:lb]
            hnew_ref[0] = h[lb - 1:lb]
        else:
            hnew_ref[j] = h[lb - 1:lb]

    zm = proj(6)
    msm = _mm(act(zm * zm), g128_ref[...])
    mq_ref[...] = (zm * lax.rsqrt(msm + EPS) * (mqg_ref[...] * mem_scale)).astype(mq_ref.dtype)
    zk = proj(4)
    msk = _mm(act(zk * zk), g64_ref[...])
    kn = zk * lax.rsqrt(msk + EPS) * kg_ref[...]

    if not transposed_kv:
        zq = proj(3)
        msq = _mm(act(zq * zq), g64_ref[...])
        q_ref[...] = (zq * lax.rsqrt(msq + EPS) * (qg_ref[...] * att_scale)).astype(q_ref.dtype)
        k_ref[...] = kn
        v_ref[...] = proj(5)
        zf = proj(7, LANES)
        lf = -_softplus(-zf)
        lf_ref[...] = lf[:, :H]
        for j in range(n_sub):
            r0 = j * lb
            cv = lf[r0:r0 + lb]
            sh = 1
            while sh < lb:
                cv = cv + jnp.where(row >= sh, pltpu.roll(cv, sh, axis=0), 0.0)
                sh *= 2
            if carry:
                cv = cv + c_carry[...]
                c_carry[...] = cv[lb - 1:lb]
            c_ref[r0:r0 + lb, :] = cv[:, :H]
    else:
        krow_ref[...] = kn.astype(krow_ref.dtype)

        def proj_t(r0, n):
            return _mm(_wparts(wt_ref, slice(r0, r0 + n), slice(None)), u, "nt") + bt_ref[r0:r0 + n, :]

        zq_t = proj_t(0, C)
        msq = _mm(g64_ref[...], act(zq_t * zq_t))
        q_ref[0] = (zq_t * lax.rsqrt(msq + EPS) * (qgc_ref[...] * att_scale)).astype(q_ref.dtype)
        zk_t = proj_t(C, C)
        msk_t = _mm(g64_ref[...], act(zk_t * zk_t))
        k_ref[0] = zk_t * lax.rsqrt(msk_t + EPS) * kgc_ref[...]
        v_ref[0] = proj_t(2 * C, C)
        lf_t = -_softplus(-proj_t(3 * C, SUBLANES_BF16))
        lf_ref[0] = lf_t[:H]
        incl = (_iota((tm, tm), 0) <= _iota((tm, tm), 1)).astype(BF16)
        c_t = _dot_exact_rhs01(lf_t, incl)
        if carry:
            c_t = c_t + c_carry[...]
            c_carry[...] = c_carry[...] + jnp.sum(lf_t, axis=1, keepdims=True)
        c_ref[0] = c_t[:H]


def _block_diag(w, size):
    n, a, _ = w.shape
    per = size // a
    w = w.reshape(n // per, per, a, a)
    eye = jnp.eye(per, dtype=w.dtype)
    return jnp.einsum('gpij,pq->gpiqj', w, eye).reshape(n // per, size, size)


def _group_mean_matrix(c, width):
    idx = np.arange(c) // width
    return jnp.asarray((idx[:, None] == idx[None, :]).astype(np.float32) / width, dtype=BF16)


def _mix_call(x, lw, pool_prev, conv_prev, h_prev, *, seq_len, pos0, H, tm, transposed_kv):
    T, D = x.shape
    C = D // 2
    lb = min(tm, seq_len)
    tiles_per_seq = max(seq_len // tm, 1)
    n_sub = tm // lb
    nseq_blk = 1 if tiles_per_seq > 1 else n_sub
    assert T % tm == 0 and tm % lb == 0 and seq_len % lb == 0 and lb % 8 == 0
    assert not transposed_kv or tiles_per_seq > 1 or seq_len == tm
    n_tiles = T // tm
    nseq = pool_prev.shape[0]
    if tiles_per_seq > 1:
        smap = lambda i: (i // tiles_per_seq, 0, 0)
    else:
        smap = lambda i: (i, 0, 0)
    tok = lambda w: pl.BlockSpec((tm, w), lambda i: (i, 0))
    tr = lambda w: pl.BlockSpec((1, w, tm), lambda i: (i // tiles_per_seq, 0, i % tiles_per_seq))
    win = lw['w_a'].shape[2]
    wtr = lw['w_t'].shape[1]
    act_dtype = F32 if lw['w_a'].shape[0] > 1 else BF16
    kern = functools.partial(_mix_kernel, C=C, H=H, tm=tm, lb=lb, tiles_per_seq=tiles_per_seq, pos0=pos0,
                             att_scale=float((C // H) ** -0.5), mem_scale=float(lw['mem_head_dim'] ** -0.5),
                             transposed_kv=transposed_kv)
    in_specs = [tok(D), _const_spec((1, D)), _const_spec(lw['w_a'].shape), _const_spec((1, win)),
                _const_spec(lw['w_t'].shape), _const_spec((wtr, 1)),
                _const_spec(lw['pool_w'].shape), _const_spec((1, C)), _const_spec((CONV_WIDTH, C)),
                _const_spec((1, C)), _const_spec(lw['wa'].shape), _const_spec((1, C)),
                _const_spec(lw['wx'].shape), _const_spec((1, C)), _const_spec((1, C)),
                _const_spec((1, C)), _const_spec((C, 1)), _const_spec((1, C)), _const_spec((C, 1)),
                _const_spec((1, C)), _const_spec((C, C)), _const_spec((C, C)),
                pl.BlockSpec((nseq_blk, POOL_HIST, C), smap), pl.BlockSpec((nseq_blk, CONV_HIST, C), smap),
                pl.BlockSpec((nseq_blk, 1, C), smap)]
    if transposed_kv:
        kv_shape = lambda w, dt: jax.ShapeDtypeStruct((nseq, w, seq_len), dt)
        kv_spec = tr
        c_rows = SUBLANES_BF16
    else:
        kv_shape = lambda w, dt: jax.ShapeDtypeStruct((T, w), dt)
        kv_spec = tok
        c_rows = 1
    out_shape = [jax.ShapeDtypeStruct((T, C), act_dtype)] * 3 + [
        kv_shape(C, act_dtype), kv_shape(C, F32), kv_shape(C, F32), kv_shape(H, F32), kv_shape(H, F32)]
    out_specs = [tok(C)] * 3 + [kv_spec(C), kv_spec(C), kv_spec(C), kv_spec(H), kv_spec(H)]
    if transposed_kv:
        out_shape.append(jax.ShapeDtypeStruct((T, C), act_dtype))
        out_specs.append(tok(C))
    out_shape += [jax.ShapeDtypeStruct((nseq, POOL_HIST, C), F32), jax.ShapeDtypeStruct((nseq, CONV_HIST, C), F32),
                  jax.ShapeDtypeStruct((nseq, 1, C), F32)]
    out_specs += [pl.BlockSpec((nseq_blk, POOL_HIST, C), smap), pl.BlockSpec((nseq_blk, CONV_HIST, C), smap),
                  pl.BlockSpec((nseq_blk, 1, C), smap)]
    scratch = [pltpu.VMEM((tm, C), F32)] * 4 + [
        pltpu.VMEM((POOL_HIST, C), F32), pltpu.VMEM((CONV_HIST, C), F32), pltpu.VMEM((1, C), F32),
        pltpu.VMEM((c_rows, 1 if transposed_kv else LANES), F32)]
    return pl.pallas_call(
        kern, grid=(n_tiles,), in_specs=in_specs, out_specs=out_specs, out_shape=out_shape,
        scratch_shapes=scratch, compiler_params=_cparams(("arbitrary",)), name="mix",
    )(x, lw['norm_mix'], lw['w_a'], lw['b_a'], lw['w_t'], lw['b_t'], lw['pool_w'], lw['pool_scale'],
      lw['conv_w'], lw['conv_b'], lw['wa'], lw['ba'], lw['wx'], lw['bx'], lw['lam'], lw['q_gain'],
      lw['q_gain_col'], lw['k_gain'], lw['k_gain_col'], lw['mq_gain'], lw['g64'], lw['g128'],
      pool_prev, conv_prev, h_prev)


def _fox_prompt_kernel(qt_ref, k_ref, vt_ref, ccol_ref, o_ref, qs_s, s_s, p_s, m_s, l_s, acc_s, *, tq, hg, dh):
    qi = pl.program_id(2)
    width = hg * dh
    n_pieces = qs_s.shape[0]
    row_h = _iota((width, 1), 0) // dh
    for i, q in enumerate(_parts(qt_ref[0])):
        for h in range(hg):
            qs_s[i, :, h * tq:(h + 1) * tq] = jnp.where(row_h == h, q, jnp.zeros_like(q))
    m_s[...] = jnp.full_like(m_s, NEG_INF)
    l_s[...] = jnp.zeros_like(l_s)
    acc_s[...] = jnp.zeros_like(acc_s)
    kpos = _iota((tq, 1), 0)
    operand = (lambda v: v) if n_pieces > 1 else (lambda v: v.astype(BF16))

    def block(kb, masked):
        start = pl.multiple_of(kb * tq, tq)
        qs = tuple(qs_s[i] for i in range(n_pieces))
        s_s[...] = _mm(k_ref[pl.ds(start, tq), :], qs)
        cc = ccol_ref[0, 0, pl.ds(start, tq), :]
        for h in range(hg):
            bias = jnp.broadcast_to(cc[:, h:h + 1], (tq, LANES))
            for j in range(tq // LANES):
                lanes = slice(h * tq + j * LANES, h * tq + (j + 1) * LANES)
                s = s_s[:, lanes] - bias
                if masked:
                    s = jnp.where(j * LANES + _iota((1, LANES), 1) >= kpos, s, NEG_INF)
                m_old = m_s[:, lanes]
                m_new = jnp.maximum(m_old, jnp.max(s, axis=0, keepdims=True))
                alpha = jnp.exp(m_old - m_new)
                p = jnp.exp(s - m_new)
                l_s[:, lanes] = alpha * l_s[:, lanes] + jnp.sum(p, axis=0, keepdims=True)
                m_s[:, lanes] = m_new
                acc_rows = slice(h * dh, (h + 1) * dh)
                acc_lanes = slice(j * LANES, (j + 1) * LANES)
                acc_s[acc_rows, acc_lanes] = acc_s[acc_rows, acc_lanes] * alpha
                for i, piece in enumerate(_parts(operand(p))):
                    p_s[i, :, lanes] = piece
        v = operand(vt_ref[0, :, pl.ds(start, tq)])
        for h in range(hg):
            ps = tuple(p_s[i, :, h * tq:(h + 1) * tq] for i in range(n_pieces))
            acc_s[h * dh:(h + 1) * dh, :] += _mm(v[h * dh:(h + 1) * dh, :], ps)

    def body(kb, c):
        block(kb, False)
        return c

    lax.fori_loop(0, qi, body, 0)
    block(qi, True)
    inv_l = 1.0 / l_s[...]
    o_t = jnp.concatenate([acc_s[h * dh:(h + 1) * dh, :] * inv_l[:, h * tq:(h + 1) * tq] for h in range(hg)], axis=0)
    o_ref[...] = o_t.T.astype(o_ref.dtype)


def _fox_prompt_call(qt, k, vt, ct, *, tq):
    B, C, S = qt.shape
    T = B * S
    n_pieces = 2 if qt.dtype == F32 else 1
    H = ct.shape[1]
    dh = C // H
    hg = MXU_DIM // dh
    ng = H // hg
    nq = S // tq
    width = hg * dh
    assert tq % LANES == 0
    ccol = jnp.transpose(ct.reshape(B, ng, hg, S), (0, 1, 3, 2))
    kern = functools.partial(_fox_prompt_kernel, tq=tq, hg=hg, dh=dh)
    return pl.pallas_call(
        kern, grid=(B, ng, nq),
        in_specs=[pl.BlockSpec((1, width, tq), lambda b, g, i: (b, g, i)),
                  pl.BlockSpec((S, width), lambda b, g, i: (b, g)),
                  pl.BlockSpec((1, width, S), lambda b, g, i: (b, g, 0)),
                  pl.BlockSpec((1, 1, S, hg), lambda b, g, i: (b, g, 0, 0))],
        out_specs=pl.BlockSpec((tq, width), lambda b, g, i: (b * nq + i, g)),
        out_shape=jax.ShapeDtypeStruct((T, C), qt.dtype),
        scratch_shapes=[pltpu.VMEM((n_pieces, width, hg * tq), BF16), pltpu.VMEM((tq, hg * tq), F32),
                        pltpu.VMEM((n_pieces, tq, hg * tq), BF16), pltpu.VMEM((1, hg * tq), F32),
                        pltpu.VMEM((1, hg * tq), F32), pltpu.VMEM((width, tq), F32)],
        compiler_params=_cparams(("arbitrary", "arbitrary", "arbitrary")), name="fox_prompt",
    )(qt, k, vt, ccol)


def _fox_decode_kernel(pt_ref, q_ref, kn_ref, vn_ref, cn_ref, *rest, pp, H, dh, L, page):
    kt_refs = rest[:pp]
    vt_refs = rest[pp:2 * pp]
    lf_refs = rest[2 * pp:3 * pp]
    o_ref = rest[3 * pp]
    kb_s, vb_s, knp_s, vnp_s, m_s, l_s, acc_s, dsum_s = rest[3 * pp + 1:]
    j = pl.program_id(1)
    nj = pl.num_programs(1)
    C = H * dh
    R = L * H

    q = q_ref[0].astype(F32)
    head_mask = (_iota((1, C), 1) // dh) == _iota((H, 1), 0)
    qs = jnp.concatenate([jnp.where(head_mask, q[t:t + 1, :], 0.0) for t in range(L)], axis=0).astype(BF16)

    @pl.when(j == 0)
    def _():
        knp_s[...] = jnp.zeros_like(knp_s)
        vnp_s[...] = jnp.zeros_like(vnp_s)
        knp_s[0:L, :] = kn_ref[0]
        vnp_s[0:L, :] = vn_ref[0]
        s = _dot_nt(qs, knp_s[...].astype(BF16)) - cn_ref[0]
        s = jnp.where(_iota((R, LANES), 1) <= _iota((R, LANES), 0) // H, s, NEG_INF)
        m_new = jnp.max(s, axis=-1, keepdims=True)
        p = jnp.exp(s - m_new)
        m_s[...] = m_new
        l_s[...] = jnp.sum(p, axis=-1, keepdims=True)
        acc_s[...] = _dot(p.astype(BF16), vnp_s[...].astype(BF16))
        dsum_s[...] = jnp.zeros_like(dsum_s)

    for i in range(pp):
        kb_s[:, i * page:(i + 1) * page] = kt_refs[i][0].astype(BF16)
        vb_s[:, i * page:(i + 1) * page] = vt_refs[i][0].astype(BF16)
    lf_all = jnp.concatenate([lf_refs[i][0] for i in range(pp)], axis=0)
    later = (_iota((page, page), 0) > _iota((page, page), 1)).astype(BF16)
    suffix = _dot_exact_rhs01(lf_all, later)
    total = jnp.sum(lf_all, axis=-1, keepdims=True)
    dsum = dsum_s[...]
    parts = []
    for i in range(pp):
        d_i = suffix[i * H:(i + 1) * H] + dsum
        parts.append(jnp.concatenate([d_i] * L, axis=0))
        dsum = dsum + total[i * H:(i + 1) * H]
    dsum_s[...] = dsum
    d = jnp.concatenate(parts, axis=1) if pp > 1 else parts[0]
    s = _dot(qs, kb_s[...]) + d
    m_old = m_s[...]
    m_new = jnp.maximum(m_old, jnp.max(s, axis=-1, keepdims=True))
    alpha = jnp.exp(m_old - m_new)
    p = jnp.exp(s - m_new)
    l_s[...] = alpha * l_s[...] + jnp.sum(p, axis=-1, keepdims=True)
    m_s[...] = m_new
    acc_s[...] = alpha * acc_s[...] + _dot_nt(p.astype(BF16), vb_s[...])

    @pl.when(j == nj - 1)
    def _():
        o = acc_s[...] * (1.0 / l_s[...])
        keep = (_iota((1, C), 1) // dh) == (_iota((R, 1), 0) % H)
        om = jnp.where(keep, o, 0.0).astype(BF16)
        pick = (_iota((SUBLANES_BF16, R), 1) // H == _iota((SUBLANES_BF16, R), 0)).astype(BF16)
        o_ref[0] = _dot(pick, om)[0:L]


def _fox_decode_call(page_table, q, k_new, v_new, c_new, kt_pages, vt_pages, lf_pages, *, page_base, pp):
    Bs, L, C = q.shape
    H = c_new.shape[-1]
    dh = C // H
    n_pages = page_table.shape[1]
    page = kt_pages.shape[2]
    assert n_pages % pp == 0 and L * H <= LANES and L <= SUBLANES_BF16
    nj = n_pages // pp
    R = L * H
    cn = jnp.broadcast_to(jnp.transpose(c_new, (0, 2, 1))[:, None], (Bs, L, H, L)).reshape(Bs, R, L)
    cn = jnp.pad(cn, ((0, 0), (0, 0), (0, LANES - L)))
    seq = lambda w: pl.BlockSpec((1, L, w), lambda b, j, pt: (b, 0, 0))

    def page_spec(i, rows):
        return pl.BlockSpec((1, rows, page),
                            lambda b, j, pt: (page_base + pt[b, n_pages - 1 - (j * pp + i)], 0, 0))

    in_specs = ([seq(C), seq(C), seq(C), pl.BlockSpec((1, R, LANES), lambda b, j, pt: (b, 0, 0))]
                + [page_spec(i, C) for i in range(pp)] + [page_spec(i, C) for i in range(pp)]
                + [page_spec(i, H) for i in range(pp)])
    kern = functools.partial(_fox_decode_kernel, pp=pp, H=H, dh=dh, L=L, page=page)
    grid_spec = pltpu.PrefetchScalarGridSpec(
        num_scalar_prefetch=1, grid=(Bs, nj), in_specs=in_specs,
        out_specs=pl.BlockSpec((1, L, C), lambda b, j, pt: (b, 0, 0)),
        scratch_shapes=[pltpu.VMEM((C, pp * page), BF16), pltpu.VMEM((C, pp * page), BF16),
                        pltpu.VMEM((LANES, C), F32), pltpu.VMEM((LANES, C), F32),
                        pltpu.VMEM((R, 1), F32), pltpu.VMEM((R, 1), F32),
                        pltpu.VMEM((R, C), F32), pltpu.VMEM((H, 1), F32)])
    return pl.pallas_call(
        kern, grid_spec=grid_spec, out_shape=jax.ShapeDtypeStruct((Bs, L, C), F32),
        compiler_params=_cparams(("arbitrary", "arbitrary")), name="fox_decode",
    )(page_table, q, k_new, v_new, cn, *([kt_pages] * pp), *([vt_pages] * pp), *([lf_pages] * pp))


def _mem_kv_kernel(m_ref, g_ref, w_ref, kg_ref, g128_ref, k_ref, v_ref, kb_ref, vb_ref, *, C):
    hp = w_ref.shape[0] > 1
    act = (lambda v: v) if hp else (lambda v: v.astype(BF16))
    u = _parts(act(_rms(m_ref[...], g_ref[...])))
    zk = _mm(u, _wparts(w_ref, slice(None), slice(0, C)))
    ms = _mm(act(zk * zk), g128_ref[...])
    kn = zk * lax.rsqrt(ms + EPS) * kg_ref[...]
    k_ref[...] = kn
    kb_ref[...] = kn.astype(BF16)
    zv = _mm(u, _wparts(w_ref, slice(None), slice(C, 2 * C)))
    v_ref[...] = zv
    vb_ref[...] = zv.astype(BF16)


def _mem_kv_call(mem, g, w, k_gain, g128, *, tm):
    T, D = mem.shape
    C = w.shape[2] // 2
    tok = lambda wd: pl.BlockSpec((tm, wd), lambda i: (i, 0))
    return pl.pallas_call(
        functools.partial(_mem_kv_kernel, C=C), grid=(T // tm,),
        in_specs=[tok(D), _const_spec((1, D)), _const_spec(w.shape), _const_spec((1, C)), _const_spec((C, C))],
        out_specs=[tok(C)] * 4,
        out_shape=[jax.ShapeDtypeStruct((T, C), F32), jax.ShapeDtypeStruct((T, C), F32),
                   jax.ShapeDtypeStruct((T, C), BF16), jax.ShapeDtypeStruct((T, C), BF16)],
        compiler_params=_cparams(("arbitrary",)), name="mem_kv",
    )(mem, g, w, k_gain, g128)


def _mem_attn_kernel(q_ref, k_ref, v_ref, o_ref, *, heads, dh):
    q = q_ref[0]
    hp = q.dtype == F32
    outs = []
    for h in range(heads):
        sl = slice(h * dh, (h + 1) * dh)
        s = _mm(q[:, sl], k_ref[0, :, sl], "nt")
        p = jnp.exp(s - jnp.max(s, axis=-1, keepdims=True))
        l = jnp.sum(p, axis=-1, keepdims=True)
        outs.append(_mm(p if hp else p.astype(BF16), v_ref[0, :, sl]) * (1.0 / l))
    o_ref[0] = jnp.concatenate(outs, axis=1).astype(o_ref.dtype)


def _mem_attn_call(q, k, v, *, heads, tq):
    B, S, C = q.shape
    M = k.shape[1]
    tq = min(tq, S)
    return pl.pallas_call(
        functools.partial(_mem_attn_kernel, heads=heads, dh=C // heads), grid=(B, S // tq),
        in_specs=[pl.BlockSpec((1, tq, C), lambda b, i: (b, i, 0)),
                  pl.BlockSpec((1, M, C), lambda b, i: (b, 0, 0)),
                  pl.BlockSpec((1, M, C), lambda b, i: (b, 0, 0))],
        out_specs=pl.BlockSpec((1, tq, C), lambda b, i: (b, i, 0)),
        out_shape=jax.ShapeDtypeStruct((B, S, C), q.dtype),
        compiler_params=_cparams(("arbitrary", "arbitrary")), name="mem_attn",
    )(q, k, v)


def _sorted_rows(tm, n_exp):
    worst = 2 * tm + n_exp * (MOE_BLOCK - 1)
    return -(-worst // MXU_DIM) * MXU_DIM


def _pick_matrix(pos_row, rows):
    return jnp.where(_iota((rows, pos_row.shape[1]), 0) == pos_row.astype(jnp.int32), 1.0, 0.0).astype(BF16)


def _merge_kernel(x_ref, op_ref, ol_ref, oa_ref, om_ref, g_ref, wg_ref, bg_ref, wb_ref, wo_ref,
                  gf_ref, wr_ref, br_ref, x1_ref, xs_ref, pos_ref, run_ref,
                  *, D, n_groups, per_group, tm, rows):
    x = x_ref[...]
    hp = wb_ref.shape[0] > 1
    act = (lambda v: v) if hp else (lambda v: v.astype(BF16))
    u = _parts(act(_rms(x, g_ref[...])))
    merged = None
    for n, o_ref in enumerate((op_ref, ol_ref, oa_ref, om_ref)):
        cols = slice(n * D, (n + 1) * D)
        gate = jax.nn.sigmoid(_mm(u, _wparts(wg_ref, slice(None), cols)) + bg_ref[:, cols])
        term = gate * _mm(o_ref[...], _wparts(wb_ref, n))
        merged = term if merged is None else merged + term
    x1 = x + _mm(act(merged), _wparts(wo_ref))
    x1_ref[...] = x1
    t = _rms(x1, gf_ref[...])
    tb = t.astype(BF16)
    t_lo = (t - tb.astype(F32)).astype(BF16)
    w_hi = wr_ref[0]
    w_lo = wr_ref[1]
    logit = _dot_nt(w_hi, tb) + _dot_nt(w_hi, t_lo) + _dot_nt(w_lo, tb) + br_ref[...]
    n_exp = n_groups * per_group
    rid = _iota(logit.shape, 0)
    big = jnp.int32(1 << 20)
    is_g = (rid >= n_exp) & (rid < n_exp + n_groups)
    gl = jnp.where(is_g, logit, NEG_INF)
    gmax = jnp.max(gl, axis=0, keepdims=True)
    gidx = jnp.min(jnp.where(gl == gmax, rid, big), axis=0, keepdims=True) - n_exp
    g_w = 1.0 / jnp.sum(jnp.where(is_g, jnp.exp(gl - gmax), 0.0), axis=0, keepdims=True)
    in_grp = (rid >= gidx * per_group) & (rid < (gidx + 1) * per_group)
    el = jnp.where(in_grp, logit, NEG_INF)
    v1 = jnp.max(el, axis=0, keepdims=True)
    i1 = jnp.min(jnp.where(el == v1, rid, big), axis=0, keepdims=True)
    el2 = jnp.where(rid == i1, NEG_INF, el)
    v2 = jnp.max(el2, axis=0, keepdims=True)
    i2 = jnp.min(jnp.where(el2 == v2, rid, big), axis=0, keepdims=True)
    e21 = jnp.exp(v2 - v1)
    w1 = g_w / (1.0 + e21)
    w2 = g_w * e21 / (1.0 + e21)

    sel1 = rid == i1
    sel2 = rid == i2
    member = jnp.where(sel1 | sel2, 1.0, 0.0)
    count = jnp.sum(member, axis=1, keepdims=True)
    padded = jnp.floor((count + (MOE_BLOCK - 1)) * (1.0 / MOE_BLOCK)) * MOE_BLOCK
    below = jnp.where(_iota((LANES, LANES), 1) < _iota((LANES, LANES), 0), 1.0, 0.0).astype(BF16)
    start = _dot(below, jnp.broadcast_to(padded, (LANES, LANES)).astype(BF16))[:, 0:1]
    earlier = jnp.where(_iota((tm, tm), 0) < _iota((tm, tm), 1), 1.0, 0.0).astype(BF16)
    rank = _dot(member.astype(BF16), earlier)
    where_to = start + rank
    pos1 = jnp.sum(jnp.where(sel1, where_to, 0.0), axis=0, keepdims=True)
    pos2 = jnp.sum(jnp.where(sel2, where_to, 0.0), axis=0, keepdims=True)
    pick1 = _pick_matrix(pos1, rows)
    pick2 = _pick_matrix(pos2, rows)
    xs_ref[:, 0:D] = _dot(pick1 + pick2, tb).astype(BF16)
    r8 = _iota((LANES, tm), 0)

    def pieces(w):
        hi, mid, lo = _split3(w)
        return jnp.where(r8 == 0, hi.astype(F32), jnp.where(r8 == 1, mid.astype(F32),
                         jnp.where(r8 == 2, lo.astype(F32), 0.0))).astype(BF16)

    xs_ref[:, D:D + LANES] = (_dot_nt(pick1, pieces(w1)) + _dot_nt(pick2, pieces(w2))).astype(BF16)
    r8p = _iota((8, tm), 0)
    pos_ref[0] = jnp.where(r8p == 0, pos1, jnp.where(r8p == 1, pos2, 0.0))
    eye = _iota((LANES, LANES), 0) == _iota((LANES, LANES), 1)
    to_row = lambda col: jnp.sum(jnp.where(eye, col, 0.0), axis=0, keepdims=True)
    r8r = _iota((8, LANES), 0)
    run_ref[0] = jnp.where(r8r == 0, to_row(padded), jnp.where(r8r == 1, to_row(start), 0.0))


def _merge_call(x, o_pool, o_lru, o_att, o_mem, lw, *, tm, n_groups, per_group):
    T, D = x.shape
    C = D // 2
    n_tiles = T // tm
    rows = _sorted_rows(tm, n_groups * per_group)
    tok = lambda w: pl.BlockSpec((tm, w), lambda i: (i, 0))
    kern = functools.partial(_merge_kernel, D=D, n_groups=n_groups, per_group=per_group, tm=tm, rows=rows)
    return pl.pallas_call(
        kern, grid=(n_tiles,),
        in_specs=[tok(D), tok(C), tok(C), tok(C), tok(C), _const_spec((1, D)), _const_spec(lw['w_g'].shape),
                  _const_spec((1, 4 * D)), _const_spec(lw['w_branch'].shape), _const_spec(lw['w_out'].shape),
                  _const_spec((1, D)),
                  _const_spec((2, LANES, D)), _const_spec((LANES, 1))],
        out_specs=[tok(D), pl.BlockSpec((rows, D + LANES), lambda i: (i, 0)),
                   pl.BlockSpec((1, 8, tm), lambda i: (i, 0, 0)), pl.BlockSpec((1, 8, LANES), lambda i: (i, 0, 0))],
        out_shape=[jax.ShapeDtypeStruct((T, D), F32), jax.ShapeDtypeStruct((n_tiles * rows, D + LANES), BF16),
                   jax.ShapeDtypeStruct((n_tiles, 8, tm), F32), jax.ShapeDtypeStruct((n_tiles, 8, LANES), F32)],
        compiler_params=_cparams(("arbitrary",)), name="merge",
    )(x, o_pool, o_lru, o_att, o_mem, lw['norm_mix'], lw['w_g'], lw['b_g'], lw['w_branch'], lw['w_out'],
      lw['norm_ffn'], lw['w_route'], lw['b_route'])


def _block_tables(run_info, *, n_exp, rows):
    n_tiles = run_info.shape[0]
    nblk = (run_info[:, 0, :n_exp] / MOE_BLOCK).astype(jnp.int32).T
    first = (run_info[:, 1, :n_exp] / MOE_BLOCK).astype(jnp.int32).T + \
        (jnp.arange(n_tiles, dtype=jnp.int32) * (rows // MOE_BLOCK))[None, :]
    nb = MOE_CHUNK_BLOCKS
    run_end = jnp.cumsum(nblk, axis=1)
    per_expert = run_end[:, -1]
    chunks_e = (per_expert + nb - 1) // nb
    chunk_end = jnp.cumsum(chunks_e)
    n_chunks = chunk_end[-1]
    max_chunks = n_tiles * (rows // MOE_BLOCK) // nb + n_exp
    cid = jnp.arange(max_chunks, dtype=jnp.int32)
    experts = jnp.arange(n_exp, dtype=jnp.int32)
    chunk_expert = jnp.sum((chunk_end[None, :] <= cid[:, None]).astype(jnp.int32), axis=1)
    last_real = jnp.max(jnp.where(chunks_e > 0, experts, 0))
    chunk_expert = jnp.where(cid < n_chunks, chunk_expert, last_real)
    mine = chunk_expert[:, None] == experts[None, :]

    def of_chunk(per_e):
        hot = mine.reshape(mine.shape + (1,) * (per_e.ndim - 1))
        return jnp.sum(jnp.where(hot, per_e[None], 0), axis=1)

    q = ((cid - of_chunk(chunk_end - chunks_e)) * nb)[:, None] + jnp.arange(nb, dtype=jnp.int32)[None, :]
    valid = (cid < n_chunks)[:, None] & (q < of_chunk(per_expert)[:, None])
    ends = of_chunk(run_end)[:, None, :]
    done = ends <= q[:, :, None]
    tile = jnp.minimum(jnp.sum(done.astype(jnp.int32), axis=2), n_tiles - 1)
    before = jnp.max(jnp.where(done, ends, 0), axis=2)
    tiles = jnp.arange(n_tiles, dtype=jnp.int32)[None, None, :]
    run_first = jnp.sum(jnp.where(tiles == tile[:, :, None], of_chunk(first)[:, None, :], 0), axis=2)
    src = jnp.where(valid, run_first + (q - before), -1).astype(jnp.int32).reshape(-1)
    return chunk_expert.astype(jnp.int32), src, n_chunks.reshape(1).astype(jnp.int32)


def _expert_kernel(ce_ref, src_ref, nc_ref, xs_in, wgu_ref, wd_ref, xs_out, xbuf, ybuf, in_sem, out_sem,
                   *, D, F):
    del xs_in
    c = pl.program_id(0)
    n = nc_ref[0]
    nb = MOE_CHUNK_BLOCKS
    blk = MOE_BLOCK

    def in_copy(chunk, slot, k):
        row = pl.multiple_of(src_ref[chunk * nb + k] * blk, blk)
        return pltpu.make_async_copy(xs_out.at[pl.ds(row, blk), :], xbuf.at[slot, pl.ds(k * blk, blk), :],
                                     in_sem.at[slot])

    def out_copy(chunk, slot, k):
        row = pl.multiple_of(src_ref[chunk * nb + k] * blk, blk)
        return pltpu.make_async_copy(ybuf.at[slot, pl.ds(k * blk, blk), :],
                                     xs_out.at[pl.ds(row, blk), pl.ds(0, D)], out_sem.at[slot])

    def for_valid(chunk, fn):
        for k in range(nb):
            @pl.when(src_ref[chunk * nb + k] >= 0)
            def _():
                fn(k)

    @pl.when(c == 0)
    def _():
        xbuf[...] = jnp.zeros_like(xbuf)
        for_valid(0, lambda k: in_copy(0, 0, k).start())

    @pl.when(c < n)
    def _():
        slot = c % 2
        for_valid(c, lambda k: in_copy(c, slot, k).wait())

        @pl.when(c + 1 < n)
        def _():
            for_valid(c + 1, lambda k: in_copy(c + 1, 1 - slot, k).start())

        @pl.when(c >= 2)
        def _():
            for_valid(c - 2, lambda k: out_copy(c - 2, slot, k).wait())

        xin = xbuf[slot]
        x = xin[:, 0:D]
        wp = xin[:, D:D + LANES].astype(F32)
        w = wp[:, 0:1] + wp[:, 1:2] + wp[:, 2:3]
        gu = _dot(x, wgu_ref[0])
        h = _silu(gu[:, :F]) * gu[:, F:]
        ybuf[slot] = _dot((h * w).astype(BF16), wd_ref[0]).astype(BF16)
        for_valid(c, lambda k: out_copy(c, slot, k).start())

        @pl.when(c == n - 1)
        def _():
            for_valid(c, lambda k: out_copy(c, slot, k).wait())

            @pl.when(c >= 1)
            def _():
                for_valid(c - 1, lambda k: out_copy(c - 1, 1 - slot, k).wait())


def _expert_call(xs, chunk_expert, src, n_chunks, wgu, wd):
    n_rows, width = xs.shape
    E, D, F2 = wgu.shape
    F = F2 // 2
    max_chunks = chunk_expert.shape[0]
    chunk_rows = MOE_CHUNK_BLOCKS * MOE_BLOCK
    grid_spec = pltpu.PrefetchScalarGridSpec(
        num_scalar_prefetch=3, grid=(max_chunks,),
        in_specs=[pl.BlockSpec(memory_space=pl.ANY),
                  pl.BlockSpec((1, D, F2), lambda c, ce, src, nc: (ce[c], 0, 0)),
                  pl.BlockSpec((1, F, D), lambda c, ce, src, nc: (ce[c], 0, 0))],
        out_specs=pl.BlockSpec(memory_space=pl.ANY),
        scratch_shapes=[pltpu.VMEM((2, chunk_rows, width), BF16), pltpu.VMEM((2, chunk_rows, D), BF16),
                        pltpu.SemaphoreType.DMA((2,)), pltpu.SemaphoreType.DMA((2,))])
    return pl.pallas_call(
        functools.partial(_expert_kernel, D=D, F=F), grid_spec=grid_spec,
        out_shape=jax.ShapeDtypeStruct((n_rows, width), BF16),
        input_output_aliases={3: 0},
        compiler_params=_cparams(("arbitrary",)), name="experts",
    )(chunk_expert, src, n_chunks, xs, wgu, wd)


def _combine_kernel(x1_ref, ys_ref, pos_ref, y_ref, *, D, rows):
    pos = pos_ref[0]
    pick = _pick_matrix(pos[0:1, :], rows) + _pick_matrix(pos[1:2, :], rows)
    y_ref[...] = x1_ref[...] + _dot_tn(pick, ys_ref[:, 0:D])


def _combine_call(x1, ys, pos, *, tm, rows):
    T, D = x1.shape
    width = ys.shape[1]
    return pl.pallas_call(
        functools.partial(_combine_kernel, D=D, rows=rows), grid=(T // tm,),
        in_specs=[pl.BlockSpec((tm, D), lambda i: (i, 0)), pl.BlockSpec((rows, width), lambda i: (i, 0)),
                  pl.BlockSpec((1, 8, tm), lambda i: (i, 0, 0))],
        out_specs=pl.BlockSpec((tm, D), lambda i: (i, 0)),
        out_shape=jax.ShapeDtypeStruct((T, D), F32),
        compiler_params=_cparams(("arbitrary",)), name="combine",
    )(x1, ys, pos)


PIECED_WEIGHTS = ('w_a', 'w_t', 'pool_w', 'wa', 'wx', 'w_mem_kv', 'w_g', 'w_branch', 'w_out')


def _single_pass(lw):
    return {k: (v[:1] if k in PIECED_WEIGHTS else v) for k, v in lw.items()}


def _prep_layer(l, p, *, C, H, mem_heads, high_precision):
    D = 2 * C
    pieces = functools.partial(_stack_parts, high_precision=high_precision)
    w_in, b_in = p['w_in'][l], p['b_in'][l]
    f0 = 6 * C
    w_a = pieces(jnp.concatenate([w_in[:, :f0], w_in[:, f0 + H:f0 + H + C], w_in[:, f0:f0 + H],
                                  jnp.zeros((D, LANES - H), F32)], axis=1))
    b_a = jnp.concatenate([b_in[:f0], b_in[f0 + H:f0 + H + C], b_in[f0:f0 + H], jnp.zeros((LANES - H,), F32)])[None]
    w_t = pieces(jnp.concatenate([w_in[:, 3 * C:6 * C].T, w_in[:, f0:f0 + H].T,
                                  jnp.zeros((SUBLANES_BF16 - H, D), F32)], axis=0))
    b_t = jnp.concatenate([b_in[3 * C:6 * C], b_in[f0:f0 + H], jnp.zeros((SUBLANES_BF16 - H,), F32)])[:, None]
    g0 = f0 + H + C
    n_exp = p['w_router'].shape[2]
    n_grp = p['w_group'].shape[2]
    w_r = jnp.concatenate([p['w_router'][l].T, p['w_group'][l].T,
                           jnp.zeros((LANES - n_exp - n_grp, D), F32)], axis=0)
    w_r_hi, w_r_lo = _stack_parts(w_r, True)
    b_r = jnp.concatenate([p['b_router'][l], p['b_group'][l], jnp.zeros((LANES - n_exp - n_grp,), F32)])[:, None]
    wg = p['w_gate'][l].reshape((n_exp,) + p['w_gate'].shape[3:])
    wu = p['w_up'][l].reshape((n_exp,) + p['w_up'].shape[3:])
    wd = p['w_down'][l].reshape((n_exp,) + p['w_down'].shape[3:])
    dh = C // H
    mdh = C // mem_heads
    k_gain = jnp.tile(p['k_norm'][l], H)
    q_gain = jnp.tile(p['q_norm'][l], H)
    return dict(
        mem_head_dim=mdh,
        norm_mix=p['norm_mix'][l][None], w_a=w_a, b_a=b_a, w_t=w_t, b_t=b_t,
        pool_w=pieces(_block_diag(p['pool_w'][l], MXU_DIM)), pool_scale=p['pool_scale'][l][None],
        conv_w=p['conv_w'][l], conv_b=p['conv_b'][l][None],
        wa=pieces(_block_diag(p['lru_wa'][l], MXU_DIM)), ba=p['lru_ba'][l][None],
        wx=pieces(_block_diag(p['lru_wx'][l], MXU_DIM)), bx=p['lru_bx'][l][None],
        lam=p['lru_lambda'][l][None],
        q_gain=q_gain[None], q_gain_col=q_gain[:, None], k_gain=k_gain[None], k_gain_col=k_gain[:, None],
        mq_gain=jnp.tile(p['mq_norm'][l], mem_heads)[None], mk_gain=jnp.tile(p['mk_norm'][l], mem_heads)[None],
        g64=_group_mean_matrix(C, dh), g128=_group_mean_matrix(C, mdh),
        mem_norm=p['mem_norm'][l][None], w_mem_kv=pieces(p['w_mem_kv'][l]),
        w_g=pieces(w_in[:, g0:]), b_g=b_in[g0:][None],
        w_branch=pieces(p['w_branch'][l]), w_out=pieces(p['w_out'][l]),
        norm_ffn=p['norm_ffn'][l][None], w_route=jnp.stack([w_r_hi, w_r_lo]), b_route=b_r,
        w_gu=jnp.concatenate([wg, wu], axis=2).astype(BF16), w_d=wd.astype(BF16),
        n_groups=n_grp, per_group=n_exp // n_grp,
    )


def _tail(x, o_pool, o_lru, o_att, o_mem, lw, *, tm):
    n_exp = lw['n_groups'] * lw['per_group']
    rows = _sorted_rows(tm, n_exp)
    x1, xs, pos, run_info = _merge_call(x, o_pool, o_lru, o_att, o_mem, lw, tm=tm,
                                        n_groups=lw['n_groups'], per_group=lw['per_group'])
    chunk_expert, src, n_chunks = _block_tables(run_info, n_exp=n_exp, rows=rows)
    ys = _expert_call(xs, chunk_expert, src, n_chunks, lw['w_gu'], lw['w_d'])
    return _combine_call(x1, ys, pos, tm=tm, rows=rows)


def kernel(x_prompt, x_sample, cache_k, cache_v, cache_logf, cache_mem_k, cache_mem_v, state_pool, state_conv,
           state_lru, page_table, mem_prompt, norm_mix, w_in, b_in, pool_w, pool_scale, conv_w, conv_b, lru_wa,
           lru_ba, lru_wx, lru_bx, lru_lambda, q_norm, k_norm, mq_norm, mem_norm, w_mem_kv, mk_norm, w_branch,
           w_out, norm_ffn, w_group, b_group, w_router, b_router, w_gate, w_up, w_down):
    p = dict(norm_mix=norm_mix, w_in=w_in, b_in=b_in, pool_w=pool_w, pool_scale=pool_scale, conv_w=conv_w,
             conv_b=conv_b, lru_wa=lru_wa, lru_ba=lru_ba, lru_wx=lru_wx, lru_bx=lru_bx, lru_lambda=lru_lambda,
             q_norm=q_norm, k_norm=k_norm, mq_norm=mq_norm, mem_norm=mem_norm, w_mem_kv=w_mem_kv, mk_norm=mk_norm,
             w_branch=w_branch, w_out=w_out, norm_ffn=norm_ffn, w_group=w_group, b_group=b_group,
             w_router=w_router, b_router=b_router, w_gate=w_gate, w_up=w_up, w_down=w_down)
    depth = w_in.shape[0]
    B, S, D = x_prompt.shape
    Bs, L, _ = x_sample.shape
    C = D // 2
    _, n_pool, page, H, dh = cache_k.shape
    M, mem_heads, mdh = cache_mem_k.shape[2:]
    n_pages = page_table.shape[1]
    past_len = n_pages * page
    n_pool_rows = state_pool.shape[2]
    n_conv_rows = state_conv.shape[2]
    Tp, Ts = B * S, Bs * L
    tm = TOKEN_TILE
    tm_s = min(tm, Ts)

    xp = x_prompt.reshape(Tp, D)
    xs = x_sample.reshape(Ts, D)
    memp = mem_prompt.reshape(B * M, D)
    pool0 = jnp.zeros((B, POOL_HIST, C), F32)
    conv0 = jnp.zeros((B, CONV_HIST, C), F32)
    h0 = jnp.zeros((B, 1, C), F32)
    kt_pages = jnp.transpose(cache_k, (0, 1, 3, 4, 2)).reshape(depth * n_pool, C, page)
    vt_pages = jnp.transpose(cache_v, (0, 1, 3, 4, 2)).reshape(depth * n_pool, C, page)
    lf_pages = jnp.transpose(cache_logf, (0, 1, 3, 2)).reshape(depth * n_pool, H, page)

    outs = {n: [] for n in ('kp', 'vp', 'lfp', 'mkp', 'mvp', 'poolp', 'convp', 'lrup',
                            'ks', 'vs', 'lfs', 'pools', 'convs', 'lrus')}
    for l in range(depth):
        hp = l < depth - 1
        lw = _prep_layer(l, p, C=C, H=H, mem_heads=mem_heads, high_precision=hp)
        lw_s = _single_pass(lw)
        mk, mv, mkb, mvb = _mem_kv_call(memp, lw['mem_norm'], lw['w_mem_kv'], lw['mk_gain'], lw['g128'], tm=512)
        (o_pool, o_lru, mq, qt, kt, vt, lft, ct, k_rows, pool_n, conv_n, h_n) = _mix_call(
            xp, lw, pool0, conv0, h0, seq_len=S, pos0=0, H=H, tm=tm, transposed_kv=True)
        o_att = _fox_prompt_call(qt, k_rows, vt, ct, tq=256)
        o_mem = _mem_attn_call(mq.reshape(B, S, C), (mk if hp else mkb).reshape(B, M, C),
                               (mv if hp else mvb).reshape(B, M, C), heads=mem_heads, tq=512).reshape(Tp, C)
        xp = _tail(xp, o_pool, o_lru, o_att, o_mem, lw, tm=tm)
        outs['kp'].append(jnp.transpose(kt.reshape(B, H, dh, S), (0, 3, 1, 2)))
        outs['vp'].append(jnp.transpose(vt.reshape(B, H, dh, S), (0, 3, 1, 2)))
        outs['lfp'].append(jnp.transpose(lft, (0, 2, 1)))
        outs['mkp'].append(mk.reshape(B, M, mem_heads, mdh)); outs['mvp'].append(mv.reshape(B, M, mem_heads, mdh))
        outs['poolp'].append(pool_n[:, POOL_HIST - n_pool_rows:]); outs['convp'].append(conv_n[:, CONV_HIST - n_conv_rows:])
        outs['lrup'].append(h_n[:, 0])
        pool_prev = jnp.pad(state_pool[l], ((0, 0), (POOL_HIST - n_pool_rows, 0), (0, 0)))
        conv_prev = jnp.pad(state_conv[l], ((0, 0), (CONV_HIST - n_conv_rows, 0), (0, 0)))
        (o_pool, o_lru, mq, q, k, v, lf, c, pool_n, conv_n, h_n) = _mix_call(
            xs, lw_s, pool_prev, conv_prev, state_lru[l][:, None], seq_len=L, pos0=past_len, H=H, tm=tm_s,
            transposed_kv=False)
        o_att = _fox_decode_call(page_table, q.reshape(Bs, L, C), k.reshape(Bs, L, C), v.reshape(Bs, L, C),
                                 c.reshape(Bs, L, H), kt_pages, vt_pages, lf_pages,
                                 page_base=l * n_pool,
                                 pp=math.gcd(n_pages, DECODE_PAGES_PER_STEP)).reshape(Ts, C).astype(BF16)
        o_mem = _mem_attn_call(mq.reshape(Bs, L, C), cache_mem_k[l].reshape(Bs, M, C).astype(BF16),
                               cache_mem_v[l].reshape(Bs, M, C).astype(BF16), heads=mem_heads, tq=L).reshape(Ts, C)
        xs = _tail(xs, o_pool, o_lru, o_att, o_mem, lw_s, tm=tm_s)
        outs['ks'].append(k.reshape(Bs, L, H, dh)); outs['vs'].append(v.reshape(Bs, L, H, dh))
        outs['lfs'].append(lf.reshape(Bs, L, H))
        outs['pools'].append(pool_n[:, POOL_HIST - n_pool_rows:]); outs['convs'].append(conv_n[:, CONV_HIST - n_conv_rows:])
        outs['lrus'].append(h_n[:, 0])

    st = lambda n: jnp.stack(outs[n])
    return (xp.reshape(B, S, D), xs.reshape(Bs, L, D),
            st('kp'), st('vp'), st('lfp'), st('mkp'), st('mvp'), st('poolp'), st('convp'), st('lrup'),
            st('ks'), st('vs'), st('lfs'), st('pools'), st('convs'), st('lrus'))
```
